```python
import jax, jax.numpy as jnp
from jax import lax
import numpy as np

D_MODEL = 1024
BATCH = 4
SEQ = 4096
DEPTH = 4

N_MIXERS = 2
N_MLA = (DEPTH + 1) // 2
N_HGRN = DEPTH // 2

MLA_HEADS = 8
MLA_Q_LORA = 512
MLA_KV_LORA = 256
MLA_NOPE = 128
MLA_ROPE = 64
MLA_V = 128
ROPE_BASE = 10000.0
ATTN_BLOCK = 128

HGRN_HEADS = 8
HGRN_DK = D_MODEL // HGRN_HEADS
HGRN_DV = D_MODEL // HGRN_HEADS
HGRN_CHUNK = 32

D_FF = 4 * D_MODEL

EPS = 1e-6

kernel_name = "hybrid_mla_hgrn2_sqrelu_sandwich"


def rms_norm(x, g):
    xf = x.astype(jnp.float32)
    y = xf * lax.rsqrt(jnp.mean(xf * xf, axis=-1, keepdims=True) + EPS)
    return (y * g.astype(jnp.float32)).astype(x.dtype)


def rope_cos_sin(positions):
    inv_freq = jnp.power(ROPE_BASE, -jnp.arange(0, MLA_ROPE, 2, dtype=jnp.float32) / MLA_ROPE)
    ang = positions.astype(jnp.float32)[..., None] * inv_freq
    return jnp.cos(ang), jnp.sin(ang)


def apply_rope(t, cos, sin):
    tf = t.astype(jnp.float32)
    t1, t2 = jnp.split(tf, 2, axis=-1)
    return jnp.concatenate([t1 * cos - t2 * sin, t1 * sin + t2 * cos], axis=-1).astype(t.dtype)


def mla_mixer(h, cos, sin, w_in, q_norm, kv_norm, w_uq, w_ukv, w_o):
    B, S, _ = h.shape
    H = MLA_HEADS
    proj = h @ w_in
    c_q, c_kv, k_r = jnp.split(proj, [MLA_Q_LORA, MLA_Q_LORA + MLA_KV_LORA], axis=-1)
    q = (rms_norm(c_q, q_norm) @ w_uq).reshape(B, S, H, MLA_NOPE + MLA_ROPE)
    q_nope = q[..., :MLA_NOPE]
    q_rope = apply_rope(q[..., MLA_NOPE:], cos[:, :, None, :], sin[:, :, None, :])
    kv = (rms_norm(c_kv, kv_norm) @ w_ukv).reshape(B, S, H, MLA_NOPE + MLA_V)
    k_nope, v = kv[..., :MLA_NOPE], kv[..., MLA_NOPE:]
    k_rope = apply_rope(k_r, cos, sin)
    scale = (MLA_NOPE + MLA_ROPE) ** -0.5
    nb = S // ATTN_BLOCK
    qn_b = q_nope.reshape(B, nb, ATTN_BLOCK, H, MLA_NOPE).transpose(1, 0, 2, 3, 4)
    qr_b = q_rope.reshape(B, nb, ATTN_BLOCK, H, MLA_ROPE).transpose(1, 0, 2, 3, 4)
    k_pos = jnp.arange(S)

    def block(args):
        qn, qr, blk = args
        s = (jnp.einsum('bqhd,bkhd->bhqk', qn, k_nope)
             + jnp.einsum('bqhr,bkr->bhqk', qr, k_rope)).astype(jnp.float32) * scale
        q_pos = blk * ATTN_BLOCK + jnp.arange(ATTN_BLOCK)
        s = jnp.where(q_pos[:, None] >= k_pos[None, :], s, -jnp.inf)
        p = jax.nn.softmax(s, axis=-1).astype(v.dtype)
        return jnp.einsum('bhqk,bkhd->bqhd', p, v)

    o = lax.map(block, (qn_b, qr_b, jnp.arange(nb)))
    o = o.transpose(1, 0, 2, 3, 4).reshape(B, S, H * MLA_V)
    return o @ w_o


def hgrn2_mixer(h, lb, w_in, o_norm, w_o):
    B, S, _ = h.shape
    H, DK, DV, C = HGRN_HEADS, HGRN_DK, HGRN_DV, HGRN_CHUNK
    nc = S // C
    HK, HV = H * DK, H * DV
    proj = h @ w_in
    q_x, f_x, i_x, g_x = jnp.split(proj, [HK, 2 * HK, 2 * HK + HV], axis=-1)

    def heads(t, d):
        return t.astype(jnp.float32).reshape(B, nc, C, H, d).transpose(0, 3, 1, 2, 4)

    f = lb + (1.0 - lb) * jax.nn.sigmoid(f_x.astype(jnp.float32))
    q = heads(jax.nn.silu(q_x.astype(jnp.float32)), DK)
    k = heads(1.0 - f, DK)
    log_f = heads(jnp.log(f), DK)
    v = heads(i_x, DV)

    b = jnp.cumsum(log_f, axis=3)
    b_ref = b[:, :, :, C // 2:C // 2 + 1, :]
    b_last = b[:, :, :, -1:, :]
    q_rel = q * jnp.exp(b - b_ref)
    k_rel = k * jnp.exp(b_ref - b)
    causal = jnp.tril(jnp.ones((C, C), dtype=bool))
    a = jnp.where(causal, jnp.einsum('bhncd,bhnsd->bhncs', q_rel, k_rel), 0.0)
    o_intra = jnp.einsum('bhncs,bhnse->bhnce', a, v)

    q_dec = q * jnp.exp(b)
    k_dec = k * jnp.exp(b_last - b)
    chunk_decay = jnp.exp(b_last[:, :, :, 0, :])

    def step(state, xs):
        qd, kd, vc, dec = xs
        o_inter = jnp.einsum('bhcd,bhde->bhce', qd, state)
        state = dec[..., None] * state + jnp.einsum('bhcd,bhce->bhde', kd, vc)
        return state, o_inter

    mv = lambda t: jnp.moveaxis(t, 2, 0)
    s0 = jnp.zeros((B, H, DK, DV), jnp.float32)
    _, o_inter = lax.scan(step, s0, (mv(q_dec), mv(k_dec), mv(v), mv(chunk_decay)))
    o = o_intra + jnp.moveaxis(o_inter, 0, 2)
    o = o.transpose(0, 2, 3, 1, 4).reshape(B, S, H, DV)
    gate = jax.nn.silu(g_x.astype(jnp.float32)).reshape(B, S, H, DV)
    o = rms_norm(o, o_norm) * gate
    return o.reshape(B, S, HV).astype(h.dtype) @ w_o


def sq_relu_mlp(h, w1, w2):
    a = jax.nn.relu(h @ w1)
    return (a * a) @ w2


def setup_inputs(seed: int = 0) -> dict:
    key = jax.random.key(seed)
    ks = jax.random.split(key, 16)
    f32 = jnp.float32
    nrm = lambda k, shape, fan_in: jax.random.normal(k, shape, f32) * (fan_in ** -0.5)
    mla_in_w = MLA_Q_LORA + MLA_KV_LORA + MLA_ROPE
    hgrn_in_w = 3 * HGRN_HEADS * HGRN_DK + HGRN_HEADS * HGRN_DV
    return {
        "x": jax.random.normal(ks[0], (BATCH, SEQ, D_MODEL), f32),
        "positions": jnp.broadcast_to(jnp.arange(SEQ, dtype=jnp.int32), (BATCH, SEQ)),
        "norm_gains": 1.0 + 0.1 * jax.random.normal(ks[1], (DEPTH, 4, D_MODEL), f32),
        "mla_w_in": nrm(ks[2], (N_MLA, D_MODEL, mla_in_w), D_MODEL),
        "mla_q_norm": 1.0 + 0.1 * jax.random.normal(ks[3], (N_MLA, MLA_Q_LORA), f32),
        "mla_kv_norm": 1.0 + 0.1 * jax.random.normal(ks[4], (N_MLA, MLA_KV_LORA), f32),
        "mla_w_uq": nrm(ks[5], (N_MLA, MLA_Q_LORA, MLA_HEADS * (MLA_NOPE + MLA_ROPE)), MLA_Q_LORA),
        "mla_w_ukv": nrm(ks[6], (N_MLA, MLA_KV_LORA, MLA_HEADS * (MLA_NOPE + MLA_V)), MLA_KV_LORA),
        "mla_w_o": nrm(ks[7], (N_MLA, MLA_HEADS * MLA_V, D_MODEL), MLA_HEADS * MLA_V),
        "hgrn_w_in": nrm(ks[8], (N_HGRN, D_MODEL, hgrn_in_w), D_MODEL),
        "hgrn_lb_logits": 0.1 * jax.random.normal(ks[9], (DEPTH, HGRN_HEADS * HGRN_DK), f32),
        "hgrn_o_norm": 1.0 + 0.1 * jax.random.normal(ks[10], (N_HGRN, HGRN_DV), f32),
        "hgrn_w_o": nrm(ks[11], (N_HGRN, HGRN_HEADS * HGRN_DV, D_MODEL), HGRN_HEADS * HGRN_DV),
        "mlp_w1": nrm(ks[12], (DEPTH, D_MODEL, D_FF), D_MODEL),
        "mlp_w2": nrm(ks[13], (DEPTH, D_FF, D_MODEL), D_FF),
    }


def reference(x, positions, norm_gains, mla_w_in, mla_q_norm, mla_kv_norm, mla_w_uq, mla_w_ukv,
              mla_w_o, hgrn_w_in, hgrn_lb_logits, hgrn_o_norm, hgrn_w_o, mlp_w1, mlp_w2):
    cos, sin = rope_cos_sin(positions)
    p = jax.nn.softmax(hgrn_lb_logits.astype(jnp.float32), axis=0)
    lower_bounds = jnp.cumsum(p, axis=0) - p[0]
    h = x
    for layer in range(DEPTH):
        slot = layer // N_MIXERS
        a = rms_norm(h, norm_gains[layer, 0])
        if layer % N_MIXERS == 0:
            m = mla_mixer(a, cos, sin, mla_w_in[slot], mla_q_norm[slot], mla_kv_norm[slot],
                          mla_w_uq[slot], mla_w_ukv[slot], mla_w_o[slot])
        else:
            m = hgrn2_mixer(a, lower_bounds[layer], hgrn_w_in[slot], hgrn_o_norm[slot], hgrn_w_o[slot])
        h = h + rms_norm(m, norm_gains[layer, 1])
        a = rms_norm(h, norm_gains[layer, 2])
        h = h + rms_norm(sq_relu_mlp(a, mlp_w1[layer], mlp_w2[layer]), norm_gains[layer, 3])
    return h
```

```python
import functools
import math

import jax
import jax.numpy as jnp
from jax import lax
from jax.experimental import pallas as pl
from jax.experimental.pallas import tpu as pltpu

D_MODEL = 1024
DEPTH = 4
N_MIXERS = 2

MLA_HEADS = 8
MLA_Q_LORA = 512
MLA_KV_LORA = 256
MLA_NOPE = 128
MLA_ROPE = 64
MLA_V = 128
ROPE_BASE = 10000.0

HGRN_HEADS = 8
HGRN_DK = D_MODEL // HGRN_HEADS
HGRN_DV = D_MODEL // HGRN_HEADS
HGRN_CHUNK = 32

D_FF = 4 * D_MODEL
EPS = 1e-6

LANES = 128
VMEM_LIMIT = 56 * 1024 * 1024

ROW_TILE = 512
ATTN_TILE = 512
HGRN_TILE = 256
FF_CHUNK = 1024

F32 = jnp.float32
BF16 = jnp.bfloat16


def _rms(x, g):
    return x * lax.rsqrt(jnp.mean(x * x, axis=-1, keepdims=True) + EPS) * g


def _dot(a, b):
    return jnp.dot(a, b, preferred_element_type=F32)


def _dot_nt(a, b):
    return lax.dot_general(a, b, (((1,), (1,)), ((), ())), preferred_element_type=F32)


def _dot_tn(a, b):
    return lax.dot_general(a, b, (((0,), (0,)), ((), ())), preferred_element_type=F32)


def _const_spec(shape):
    return pl.BlockSpec(shape, lambda *_: (0,) * len(shape))


def _params(*sem):
    return pltpu.CompilerParams(dimension_semantics=sem, vmem_limit_bytes=VMEM_LIMIT)


def _mla_proj_kernel(h_ref, g_ref, win_ref, qg_ref, kvg_ref, wuq_ref, wukv_ref, cos_ref, sin_ref,
                     qn_ref, qr_ref, kf_ref, v_ref, *, q_scale):
    nq, nkv = MLA_Q_LORA, MLA_KV_LORA
    hn = MLA_HEADS * MLA_NOPE
    a = _rms(h_ref[...], g_ref[...]).astype(BF16)
    proj = _dot(a, win_ref[...])
    c_q = _rms(proj[:, :nq], qg_ref[...]).astype(BF16)
    c_kv = _rms(proj[:, nq:nq + nkv], kvg_ref[...]).astype(BF16)
    cos = cos_ref[...]
    sin = sin_ref[...]
    k_rope = proj[:, nq + nkv:nq + nkv + LANES] * cos + proj[:, nq + nkv + LANES:] * sin
    q = _dot(c_q, wuq_ref[...])
    qn_ref[...] = (q[:, :hn] * q_scale).astype(BF16)
    n_pair = MLA_HEADS * MLA_ROPE // LANES
    for p in range(n_pair):
        r = q[:, hn + p * LANES:hn + (p + 1) * LANES]
        rs = q[:, hn + (n_pair + p) * LANES:hn + (n_pair + p + 1) * LANES]
        qr_ref[:, p * LANES:(p + 1) * LANES] = ((r * cos + rs * sin) * q_scale).astype(BF16)
    kv = _dot(c_kv, wukv_ref[...])
    v_ref[...] = kv[:, hn:].astype(BF16)
    lane = lax.broadcasted_iota(jnp.int32, k_rope.shape, 1)
    kr_even = jnp.where(lane < MLA_ROPE, k_rope, 0.0).astype(BF16)
    kr_odd = jnp.where(lane >= MLA_ROPE, k_rope, 0.0).astype(BF16)
    for hd in range(MLA_HEADS):
        kf_ref[:, 2 * hd * LANES:(2 * hd + 1) * LANES] = kv[:, hd * LANES:(hd + 1) * LANES].astype(BF16)
        kf_ref[:, (2 * hd + 1) * LANES:(2 * hd + 2) * LANES] = kr_even if hd % 2 == 0 else kr_odd


def _mla_proj(h, gain, w_in, q_gain, kv_gain, w_uq, w_ukv, cos, sin, q_scale):
    t = h.shape[0]
    tm = ROW_TILE
    row = lambda w: pl.BlockSpec((tm, w), lambda i: (i, 0))
    hn = MLA_HEADS * MLA_NOPE
    return pl.pallas_call(
        functools.partial(_mla_proj_kernel, q_scale=q_scale),
        grid=(t // tm,),
        in_specs=[row(D_MODEL), _const_spec(gain.shape), _const_spec(w_in.shape), _const_spec(q_gain.shape),
                  _const_spec(kv_gain.shape), _const_spec(w_uq.shape), _const_spec(w_ukv.shape),
                  row(LANES), row(LANES)],
        out_specs=[row(hn), row(MLA_HEADS * MLA_ROPE), row(2 * hn), row(hn)],
        out_shape=[jax.ShapeDtypeStruct((t, hn), BF16), jax.ShapeDtypeStruct((t, MLA_HEADS * MLA_ROPE), BF16),
                   jax.ShapeDtypeStruct((t, 2 * hn), BF16), jax.ShapeDtypeStruct((t, hn), BF16)],
        compiler_params=_params("parallel"),
        name="mla_proj",
    )(h, gain, w_in, q_gain, kv_gain, w_uq, w_ukv, cos, sin)


def _attn_kernel(qn_ref, qr_ref, k_ref, v_ref, o_ref):
    tq = ATTN_TILE
    i = pl.program_id(2)
    q = jnp.concatenate([qn_ref[0], qr_ref[0]], axis=-1)

    def step(j, carry, masked):
        m, l, acc = carry
        start = pl.multiple_of(j * tq, tq)
        s = _dot_nt(q, k_ref[0, pl.ds(start, tq), :])
        if masked:
            row = lax.broadcasted_iota(jnp.int32, s.shape, 0)
            col = lax.broadcasted_iota(jnp.int32, s.shape, 1)
            s = jnp.where(row >= col, s, -jnp.inf)
        m_new = jnp.maximum(m, jnp.max(s, axis=-1, keepdims=True))
        alpha = jnp.exp2(m - m_new)
        p = jnp.exp2(s - m_new)
        l = alpha * l + jnp.sum(p, axis=-1, keepdims=True)
        acc = alpha * acc + _dot(p.astype(BF16), v_ref[0, pl.ds(start, tq), :])
        return m_new, l, acc

    init = (jnp.full((tq, 1), -jnp.inf, F32), jnp.zeros((tq, 1), F32), jnp.zeros((tq, MLA_V), F32))
    carry = lax.fori_loop(0, i, functools.partial(step, masked=False), init)
    _, l, acc = step(i, carry, masked=True)
    o_ref[0] = (acc / l).astype(o_ref.dtype)


def _attention(qn, qr, kf, v):
    b, s, _ = qn.shape
    tq = ATTN_TILE
    return pl.pallas_call(
        _attn_kernel,
        grid=(b, MLA_HEADS, s // tq),
        in_specs=[pl.BlockSpec((1, tq, LANES), lambda bi, hd, i: (bi, i, hd)),
                  pl.BlockSpec((1, tq, LANES), lambda bi, hd, i: (bi, i, hd // 2)),
                  pl.BlockSpec((1, s, 2 * LANES), lambda bi, hd, i: (bi, 0, hd)),
                  pl.BlockSpec((1, s, LANES), lambda bi, hd, i: (bi, 0, hd))],
        out_specs=pl.BlockSpec((1, tq, LANES), lambda bi, hd, i: (bi, i, hd)),
        out_shape=jax.ShapeDtypeStruct((b, s, MLA_HEADS * MLA_V), BF16),
        compiler_params=_params("parallel", "parallel", "arbitrary"),
        name="mla_attention",
    )(qn, qr, kf, v)


def _hgrn_proj_kernel(h_ref, g_ref, win_ref, lb_ref, q_ref, f_ref, v_ref, gate_ref):
    hk = HGRN_HEADS * HGRN_DK
    a = _rms(h_ref[...], g_ref[...]).astype(BF16)
    lb = lb_ref[...]
    qx = _dot(a, win_ref[:, :hk])
    q_ref[...] = qx * jax.nn.sigmoid(qx)
    fx = _dot(a, win_ref[:, hk:2 * hk])
    f_ref[...] = lb + (1.0 - lb) * jax.nn.sigmoid(fx)
    v_ref[...] = _dot(a, win_ref[:, 2 * hk:3 * hk]).astype(BF16)
    gx = _dot(a, win_ref[:, 3 * hk:])
    gate_ref[...] = (gx * jax.nn.sigmoid(gx)).astype(BF16)


def _hgrn_proj(h, gain, w_in, lb):
    t = h.shape[0]
    tm = ROW_TILE
    hk = HGRN_HEADS * HGRN_DK
    row = pl.BlockSpec((tm, hk), lambda i: (i, 0))
    return pl.pallas_call(
        _hgrn_proj_kernel,
        grid=(t // tm,),
        in_specs=[row, _const_spec(gain.shape), _const_spec(w_in.shape), _const_spec(lb.shape)],
        out_specs=[row, row, row, row],
        out_shape=[jax.ShapeDtypeStruct((t, hk), F32), jax.ShapeDtypeStruct((t, hk), F32),
                   jax.ShapeDtypeStruct((t, hk), BF16), jax.ShapeDtypeStruct((t, hk), BF16)],
        compiler_params=_params("parallel"),
        name="hgrn_proj",
    )(h, gain, w_in, lb)


def _chunk_cumsum(x, pos):
    shift = 1
    while shift < HGRN_CHUNK:
        x = x + jnp.where(pos >= shift, pltpu.roll(x, shift, 0), 0.0)
        shift *= 2
    return x


def _hgrn_rec_kernel(q_ref, f_ref, v_ref, gate_ref, on_ref, y_ref, state_ref):
    c = HGRN_CHUNK
    tl = HGRN_TILE
    nsub = tl // c

    @pl.when(pl.program_id(2) == 0)
    def _():
        state_ref[...] = jnp.zeros_like(state_ref)

    q = q_ref[0]
    f = f_ref[0]
    v = v_ref[0]
    k = 1.0 - f
    pos = lax.broadcasted_iota(jnp.int32, (tl, HGRN_DK), 0) % c
    b = _chunk_cumsum(jnp.log(f), pos)

    def chunk_row(r):
        return jnp.concatenate([jnp.broadcast_to(b[n * c + r:n * c + r + 1, :], (c, HGRN_DK))
                                for n in range(nsub)], axis=0)

    b_ref = chunk_row(c // 2)
    b_last = chunk_row(c - 1)

    q_rel = (q * jnp.exp(b - b_ref)).astype(BF16)
    k_rel = (k * jnp.exp(b_ref - b)).astype(BF16)
    a = _dot_nt(q_rel, k_rel)
    row = lax.broadcasted_iota(jnp.int32, a.shape, 0)
    col = lax.broadcasted_iota(jnp.int32, a.shape, 1)
    keep = (row // c == col // c) & (row >= col)
    o = _dot(jnp.where(keep, a, 0.0).astype(BF16), v)

    q_dec = (q * jnp.exp(b)).astype(BF16)
    k_dec = (k * jnp.exp(b_last - b)).astype(BF16)

    st = state_ref[...]
    inter = []
    for n in range(nsub):
        rows = slice(n * c, (n + 1) * c)
        inter.append(_dot_nt(q_dec[rows], st.astype(BF16)))
        chunk_decay = jnp.exp(b[(n + 1) * c - 1:(n + 1) * c, :])
        st = st * chunk_decay + _dot_tn(v[rows], k_dec[rows])
    state_ref[...] = st
    o = o + jnp.concatenate(inter, axis=0)

    y = _rms(o, on_ref[...]) * gate_ref[0].astype(F32)
    y_ref[0] = y.astype(y_ref.dtype)


def _hgrn_rec(q, f, v, gate, o_gain):
    b, s, _ = q.shape
    blk = pl.BlockSpec((1, HGRN_TILE, HGRN_DK), lambda bi, hd, i: (bi, i, hd))
    return pl.pallas_call(
        _hgrn_rec_kernel,
        grid=(b, HGRN_HEADS, s // HGRN_TILE),
        in_specs=[blk, blk, blk, blk, _const_spec(o_gain.shape)],
        out_specs=blk,
        out_shape=jax.ShapeDtypeStruct((b, s, HGRN_HEADS * HGRN_DV), BF16),
        scratch_shapes=[pltpu.VMEM((HGRN_DV, HGRN_DK), F32)],
        compiler_params=_params("parallel", "parallel", "arbitrary"),
        name="hgrn_recurrence",
    )(q, f, v, gate, o_gain)


def _post_mlp_kernel(h_ref, y_ref, wo_ref, g_ref, w1_ref, w2_ref, out_ref, u_ref):
    g = g_ref[...]
    h1 = h_ref[...] + _rms(_dot(y_ref[...], wo_ref[...]), g[1:2])
    a = _rms(h1, g[2:3]).astype(BF16)
    for c in range(D_FF // FF_CHUNK):
        cols = slice(c * FF_CHUNK, (c + 1) * FF_CHUNK)
        u = jnp.maximum(_dot(a, w1_ref[:, cols]), 0.0)
        u_ref[:, cols] = (u * u).astype(BF16)
    out_ref[...] = h1 + _rms(_dot(u_ref[...], w2_ref[...]), g[3:4])


def _post_mlp(h, y, w_o, gains, w1, w2):
    t = h.shape[0]
    tm = ROW_TILE
    row = pl.BlockSpec((tm, D_MODEL), lambda i: (i, 0))
    once = lambda shape: pl.BlockSpec(shape, lambda i: (0,) * len(shape), pipeline_mode=pl.Buffered(1))
    return pl.pallas_call(
        _post_mlp_kernel,
        grid=(t // tm,),
        in_specs=[row, row, once(w_o.shape), _const_spec(gains.shape), once(w1.shape), once(w2.shape)],
        out_specs=row,
        out_shape=jax.ShapeDtypeStruct((t, D_MODEL), F32),
        scratch_shapes=[pltpu.VMEM((tm, D_FF), BF16)],
        compiler_params=_params("parallel"),
        name="post_mlp",
    )(h, y, w_o, gains, w1, w2)


def _swap_halves(w):
    half = MLA_ROPE // 2
    g = w.reshape(w.shape[0], -1, 2, half)
    return g[:, :, ::-1, :].reshape(w.shape)


def _mla_weights(w_in, w_uq, w_ukv):
    nq, nkv = MLA_Q_LORA, MLA_KV_LORA
    kr = w_in[:, nq + nkv:]
    krs = _swap_halves(kr)
    w_in_x = jnp.concatenate([w_in[:, :nq + nkv], kr, kr, krs, krs], axis=1).astype(BF16)
    uq = w_uq.reshape(nq, MLA_HEADS, MLA_NOPE + MLA_ROPE)
    uq_nope = uq[:, :, :MLA_NOPE].reshape(nq, -1)
    uq_rope = uq[:, :, MLA_NOPE:].reshape(nq, -1)
    w_uq_x = jnp.concatenate([uq_nope, uq_rope, _swap_halves(uq_rope)], axis=1).astype(BF16)
    ukv = w_ukv.reshape(nkv, MLA_HEADS, MLA_NOPE + MLA_V)
    w_ukv_x = jnp.concatenate([ukv[:, :, :MLA_NOPE].reshape(nkv, -1), ukv[:, :, MLA_NOPE:].reshape(nkv, -1)],
                              axis=1).astype(BF16)
    return w_in_x, w_uq_x, w_ukv_x


def kernel(x, positions, norm_gains, mla_w_in, mla_q_norm, mla_kv_norm, mla_w_uq, mla_w_ukv, mla_w_o, hgrn_w_in,
           hgrn_lb_logits, hgrn_o_norm, hgrn_w_o, mlp_w1, mlp_w2):
    bsz, seq, d = x.shape
    t = bsz * seq

    inv_freq = jnp.power(ROPE_BASE, -jnp.arange(0, MLA_ROPE, 2, dtype=F32) / MLA_ROPE)
    ang = positions.astype(F32).reshape(t, 1) * inv_freq
    cos, sin = jnp.cos(ang), jnp.sin(ang)
    cos_t = jnp.concatenate([cos, cos, cos, cos], axis=1)
    sin_t = jnp.concatenate([-sin, sin, -sin, sin], axis=1)
    p = jax.nn.softmax(hgrn_lb_logits.astype(F32), axis=0)
    lower_bounds = jnp.cumsum(p, axis=0) - p[0]
    q_scale = (MLA_NOPE + MLA_ROPE) ** -0.5 * math.log2(math.e)

    h = x.reshape(t, d)
    for layer in range(DEPTH):
        slot = layer // N_MIXERS
        gains = norm_gains[layer]
        if layer % N_MIXERS == 0:
            w_in, w_uq, w_ukv = _mla_weights(mla_w_in[slot], mla_w_uq[slot], mla_w_ukv[slot])
            qn, qr, kf, v = _mla_proj(h, gains[0:1], w_in, mla_q_norm[slot][None], mla_kv_norm[slot][None],
                                      w_uq, w_ukv, cos_t, sin_t, q_scale)
            y = _attention(qn.reshape(bsz, seq, -1), qr.reshape(bsz, seq, -1), kf.reshape(bsz, seq, -1),
                           v.reshape(bsz, seq, -1))
            w_o = mla_w_o[slot]
        else:
            q, f, v, gate = _hgrn_proj(h, gains[0:1], hgrn_w_in[slot].astype(BF16), lower_bounds[layer][None])
            shp = (bsz, seq, -1)
            y = _hgrn_rec(q.reshape(shp), f.reshape(shp), v.reshape(shp), gate.reshape(shp),
                          hgrn_o_norm[slot][None])
            w_o = hgrn_w_o[slot]
        h = _post_mlp(h, y.reshape(t, -1), w_o.astype(BF16), gains, mlp_w1[layer].astype(BF16),
                      mlp_w2[layer].astype(BF16))
    return h.reshape(bsz, seq, d)
```

```python
import functools
import math

import jax
import jax.numpy as jnp
from jax import lax
from jax.experimental import pallas as pl
from jax.experimental.pallas import tpu as pltpu

D_MODEL = 1024
DEPTH = 4
N_MIXERS = 2

MLA_HEADS = 8
MLA_Q_LORA = 512
MLA_KV_LORA = 256
MLA_NOPE = 128
MLA_ROPE = 64
MLA_V = 128
ROPE_BASE = 10000.0

HGRN_HEADS = 8
HGRN_DK = D_MODEL // HGRN_HEADS
HGRN_DV = D_MODEL // HGRN_HEADS
HGRN_CHUNK = 32

D_FF = 4 * D_MODEL
EPS = 1e-6

LANES = 128
VMEM_LIMIT = 56 * 1024 * 1024

ROW_TILE = 512
ATTN_TILE = 256
ATTN_HEADS = 2
HGRN_TILE = 256
FF_CHUNK = 1024

F32 = jnp.float32
BF16 = jnp.bfloat16


def _rms(x, g):
    return x * lax.rsqrt(jnp.mean(x * x, axis=-1, keepdims=True) + EPS) * g


def _dot(a, b):
    return jnp.dot(a, b, preferred_element_type=F32)


def _dot_nt(a, b):
    return lax.dot_general(a, b, (((1,), (1,)), ((), ())), preferred_element_type=F32)


def _dot_tn(a, b):
    return lax.dot_general(a, b, (((0,), (0,)), ((), ())), preferred_element_type=F32)


def _const_spec(shape):
    return pl.BlockSpec(shape, lambda *_: (0,) * len(shape))


def _params(*sem):
    return pltpu.CompilerParams(dimension_semantics=sem, vmem_limit_bytes=VMEM_LIMIT)


def _mla_proj_kernel(h_ref, g_ref, win_ref, qg_ref, kvg_ref, wuq_ref, wukv_ref, cos_ref, sin_ref,
                     qn_ref, qr_ref, kf_ref, vt_ref, *, q_scale):
    nq, nkv = MLA_Q_LORA, MLA_KV_LORA
    hn = MLA_HEADS * MLA_NOPE
    a = _rms(h_ref[...], g_ref[...]).astype(BF16)
    proj = _dot(a, win_ref[...])
    c_q = _rms(proj[:, :nq], qg_ref[...]).astype(BF16)
    c_kv = _rms(proj[:, nq:nq + nkv], kvg_ref[...]).astype(BF16)
    cos = cos_ref[...]
    sin = sin_ref[...]
    k_rope = proj[:, nq + nkv:nq + nkv + LANES] * cos + proj[:, nq + nkv + LANES:] * sin
    q = _dot(c_q, wuq_ref[...])
    qn_ref[...] = (q[:, :hn] * q_scale).astype(BF16)
    n_pair = MLA_HEADS * MLA_ROPE // LANES
    for p in range(n_pair):
        r = q[:, hn + p * LANES:hn + (p + 1) * LANES]
        rs = q[:, hn + (n_pair + p) * LANES:hn + (n_pair + p + 1) * LANES]
        qr_ref[:, p * LANES:(p + 1) * LANES] = ((r * cos + rs * sin) * q_scale).astype(BF16)
    kv = _dot(c_kv, wukv_ref[...])
    vt = kv[:, hn:].T.astype(BF16)
    for c in range(vt_ref.shape[0]):
        vt_ref[c] = vt[:, c * ATTN_TILE:(c + 1) * ATTN_TILE]
    lane = lax.broadcasted_iota(jnp.int32, k_rope.shape, 1)
    kr_even = jnp.where(lane < MLA_ROPE, k_rope, 0.0).astype(BF16)
    kr_odd = jnp.where(lane >= MLA_ROPE, k_rope, 0.0).astype(BF16)
    for hd in range(MLA_HEADS):
        kf_ref[:, 2 * hd * LANES:(2 * hd + 1) * LANES] = kv[:, hd * LANES:(hd + 1) * LANES].astype(BF16)
        kf_ref[:, (2 * hd + 1) * LANES:(2 * hd + 2) * LANES] = kr_even if hd % 2 == 0 else kr_odd


def _mla_proj(h, gain, w_in, q_gain, kv_gain, w_uq, w_ukv, cos, sin, q_scale):
    t = h.shape[0]
    tm = ROW_TILE
    row = lambda w: pl.BlockSpec((tm, w), lambda i: (i, 0))
    hn = MLA_HEADS * MLA_NOPE
    return pl.pallas_call(
        functools.partial(_mla_proj_kernel, q_scale=q_scale),
        grid=(t // tm,),
        in_specs=[row(D_MODEL), _const_spec(gain.shape), _const_spec(w_in.shape), _const_spec(q_gain.shape),
                  _const_spec(kv_gain.shape), _const_spec(w_uq.shape), _const_spec(w_ukv.shape),
                  row(LANES), row(LANES)],
        out_specs=[row(hn), row(MLA_HEADS * MLA_ROPE), row(2 * hn),
                   pl.BlockSpec((tm // ATTN_TILE, hn, ATTN_TILE), lambda i: (i, 0, 0))],
        out_shape=[jax.ShapeDtypeStruct((t, hn), BF16), jax.ShapeDtypeStruct((t, MLA_HEADS * MLA_ROPE), BF16),
                   jax.ShapeDtypeStruct((t, 2 * hn), BF16),
                   jax.ShapeDtypeStruct((t // ATTN_TILE, hn, ATTN_TILE), BF16)],
        compiler_params=_params("parallel"),
        name="mla_proj",
    )(h, gain, w_in, q_gain, kv_gain, w_uq, w_ukv, cos, sin)


def _attn_kernel(qn_ref, qr_ref, k_ref, vt_ref, o_ref, qt_ref, sa_ref, sb_ref, pa_ref, pb_ref, aa_ref, ab_ref,
                 m_ref, l_ref, acc_ref):
    tq = ATTN_TILE
    heads = range(ATTN_HEADS)
    i = pl.program_id(2)
    for hd in heads:
        qt_ref[hd, :LANES, :] = qn_ref[0, :, hd * LANES:(hd + 1) * LANES].T
        qt_ref[hd, LANES:, :] = qr_ref[0, :, (hd // 2) * LANES:(hd // 2 + 1) * LANES].T
    m_ref[...] = jnp.full(m_ref.shape, -jnp.inf, F32)
    l_ref[...] = jnp.zeros(l_ref.shape, F32)
    acc_ref[...] = jnp.zeros(acc_ref.shape, F32)
    pb_ref[...] = jnp.zeros(pb_ref.shape, BF16)
    ab_ref[...] = jnp.ones(ab_ref.shape, F32)

    def scores(j, st_ref):
        start = pl.multiple_of(j * tq, tq)
        for hd in heads:
            st_ref[hd] = _dot(k_ref[0, pl.ds(start, tq), 2 * hd * LANES:(2 * hd + 2) * LANES], qt_ref[hd])

    def softmax(st_ref, pt_ref, alpha_ref, masked=False):
        for hd in heads:
            st = st_ref[hd]
            if masked:
                key = lax.broadcasted_iota(jnp.int32, st.shape, 0)
                qry = lax.broadcasted_iota(jnp.int32, st.shape, 1)
                st = jnp.where(qry >= key, st, -jnp.inf)
            m = m_ref[hd]
            m_new = jnp.maximum(m, jnp.max(st, axis=0, keepdims=True))
            alpha = jnp.exp2(m - m_new)
            pt = jnp.exp2(st - m_new)
            m_ref[hd] = m_new
            l_ref[hd] = alpha * l_ref[hd] + jnp.sum(pt, axis=0, keepdims=True)
            pt_ref[hd] = pt.astype(BF16)
            alpha_ref[hd] = alpha

    def accumulate(j, pt_ref, alpha_ref):
        for hd in heads:
            pv = _dot(vt_ref[0, j, hd * MLA_V:(hd + 1) * MLA_V, :], pt_ref[hd])
            acc_ref[hd] = alpha_ref[hd] * acc_ref[hd] + pv

    def pair(p, carry):
        scores(2 * p + 1, sb_ref)
        accumulate(jnp.maximum(2 * p - 1, 0), pb_ref, ab_ref)
        softmax(sa_ref, pa_ref, aa_ref)
        scores(2 * p + 2, sa_ref)
        accumulate(2 * p, pa_ref, aa_ref)
        softmax(sb_ref, pb_ref, ab_ref)
        return carry

    scores(0, sa_ref)
    lax.fori_loop(0, i // 2, pair, 0)

    @pl.when(i % 2 == 0)
    def _():
        accumulate(jnp.maximum(i - 1, 0), pb_ref, ab_ref)
        softmax(sa_ref, pa_ref, aa_ref, masked=True)
        accumulate(i, pa_ref, aa_ref)

    @pl.when(i % 2 == 1)
    def _():
        scores(i, sb_ref)
        accumulate(jnp.maximum(i - 2, 0), pb_ref, ab_ref)
        softmax(sa_ref, pa_ref, aa_ref)
        accumulate(i - 1, pa_ref, aa_ref)
        softmax(sb_ref, pb_ref, ab_ref, masked=True)
        accumulate(i, pb_ref, ab_ref)

    for hd in heads:
        o_ref[0, :, hd * MLA_V:(hd + 1) * MLA_V] = (acc_ref[hd] / l_ref[hd]).T.astype(o_ref.dtype)


def _attention(qn, qr, kf, vt):
    b, s, _ = qn.shape
    tq = ATTN_TILE
    nh = ATTN_HEADS
    return pl.pallas_call(
        _attn_kernel,
        grid=(b, MLA_HEADS // nh, s // tq),
        in_specs=[pl.BlockSpec((1, tq, nh * LANES), lambda bi, hg, i: (bi, i, hg)),
                  pl.BlockSpec((1, tq, nh * LANES // 2), lambda bi, hg, i: (bi, i, hg)),
                  pl.BlockSpec((1, s, nh * 2 * LANES), lambda bi, hg, i: (bi, 0, hg)),
                  pl.BlockSpec((1, s // tq, nh * MLA_V, tq), lambda bi, hg, i: (bi, 0, hg, 0))],
        out_specs=pl.BlockSpec((1, tq, nh * MLA_V), lambda bi, hg, i: (bi, i, hg)),
        out_shape=jax.ShapeDtypeStruct((b, s, MLA_HEADS * MLA_V), BF16),
        scratch_shapes=[pltpu.VMEM((nh, 2 * LANES, tq), BF16), pltpu.VMEM((nh, tq, tq), F32),
                        pltpu.VMEM((nh, tq, tq), F32), pltpu.VMEM((nh, tq, tq), BF16),
                        pltpu.VMEM((nh, tq, tq), BF16), pltpu.VMEM((nh, 1, tq), F32), pltpu.VMEM((nh, 1, tq), F32),
                        pltpu.VMEM((nh, 1, tq), F32), pltpu.VMEM((nh, 1, tq), F32),
                        pltpu.VMEM((nh, MLA_V, tq), F32)],
        compiler_params=_params("parallel", "parallel", "arbitrary"),
        name="mla_attention",
    )(qn, qr, kf, vt)


def _hgrn_proj_kernel(h_ref, g_ref, win_ref, lb_ref, q_ref, f_ref, v_ref, gate_ref):
    hk = HGRN_HEADS * HGRN_DK
    a = _rms(h_ref[...], g_ref[...]).astype(BF16)
    lb = lb_ref[...]
    qx = _dot(a, win_ref[:, :hk])
    q_ref[...] = qx * jax.nn.sigmoid(qx)
    fx = _dot(a, win_ref[:, hk:2 * hk])
    f_ref[...] = lb + (1.0 - lb) * jax.nn.sigmoid(fx)
    v_ref[...] = _dot(a, win_ref[:, 2 * hk:3 * hk]).astype(BF16)
    gx = _dot(a, win_ref[:, 3 * hk:])
    gate_ref[...] = (gx * jax.nn.sigmoid(gx)).astype(BF16)


def _hgrn_proj(h, gain, w_in, lb):
    t = h.shape[0]
    tm = ROW_TILE
    hk = HGRN_HEADS * HGRN_DK
    row = pl.BlockSpec((tm, hk), lambda i: (i, 0))
    return pl.pallas_call(
        _hgrn_proj_kernel,
        grid=(t // tm,),
        in_specs=[row, _const_spec(gain.shape), _const_spec(w_in.shape), _const_spec(lb.shape)],
        out_specs=[row, row, row, row],
        out_shape=[jax.ShapeDtypeStruct((t, hk), F32), jax.ShapeDtypeStruct((t, hk), F32),
                   jax.ShapeDtypeStruct((t, hk), BF16), jax.ShapeDtypeStruct((t, hk), BF16)],
        compiler_params=_params("parallel"),
        name="hgrn_proj",
    )(h, gain, w_in, lb)


def _chunk_cumsum(x, pos):
    shift = 1
    while shift < HGRN_CHUNK:
        x = x + jnp.where(pos >= shift, pltpu.roll(x, shift, 0), 0.0)
        shift *= 2
    return x


def _hgrn_rec_kernel(q_ref, f_ref, v_ref, gate_ref, on_ref, y_ref, state_ref):
    c = HGRN_CHUNK
    tl = HGRN_TILE
    nsub = tl // c

    @pl.when(pl.program_id(2) == 0)
    def _():
        state_ref[...] = jnp.zeros_like(state_ref)

    q = q_ref[0]
    f = f_ref[0]
    v = v_ref[0]
    k = 1.0 - f
    pos = lax.broadcasted_iota(jnp.int32, (tl, HGRN_DK), 0) % c
    b = _chunk_cumsum(jnp.log(f), pos)

    def chunk_row(r):
        return jnp.concatenate([jnp.broadcast_to(b[n * c + r:n * c + r + 1, :], (c, HGRN_DK))
                                for n in range(nsub)], axis=0)

    b_ref = chunk_row(c // 2)
    b_last = chunk_row(c - 1)

    q_rel = (q * jnp.exp(b - b_ref)).astype(BF16)
    k_rel = (k * jnp.exp(b_ref - b)).astype(BF16)
    a = _dot_nt(q_rel, k_rel)
    row = lax.broadcasted_iota(jnp.int32, a.shape, 0)
    col = lax.broadcasted_iota(jnp.int32, a.shape, 1)
    keep = (row // c == col // c) & (row >= col)
    o = _dot(jnp.where(keep, a, 0.0).astype(BF16), v)

    q_dec = (q * jnp.exp(b)).astype(BF16)
    k_dec = (k * jnp.exp(b_last - b)).astype(BF16)

    st = state_ref[...]
    inter = []
    for n in range(nsub):
        rows = slice(n * c, (n + 1) * c)
        inter.append(_dot_nt(q_dec[rows], st.astype(BF16)))
        chunk_decay = jnp.exp(b[(n + 1) * c - 1:(n + 1) * c, :])
        st = st * chunk_decay + _dot_tn(v[rows], k_dec[rows])
    state_ref[...] = st
    o = o + jnp.concatenate(inter, axis=0)

    y = _rms(o, on_ref[...]) * gate_ref[0].astype(F32)
    y_ref[0] = y.astype(y_ref.dtype)


def _hgrn_rec(q, f, v, gate, o_gain):
    b, s, _ = q.shape
    blk = pl.BlockSpec((1, HGRN_TILE, HGRN_DK), lambda bi, hd, i: (bi, i, hd))
    return pl.pallas_call(
        _hgrn_rec_kernel,
        grid=(b, HGRN_HEADS, s // HGRN_TILE),
        in_specs=[blk, blk, blk, blk, _const_spec(o_gain.shape)],
        out_specs=blk,
        out_shape=jax.ShapeDtypeStruct((b, s, HGRN_HEADS * HGRN_DV), BF16),
        scratch_shapes=[pltpu.VMEM((HGRN_DV, HGRN_DK), F32)],
        compiler_params=_params("parallel", "parallel", "arbitrary"),
        name="hgrn_recurrence",
    )(q, f, v, gate, o_gain)


def _post_mlp_kernel(h_ref, y_ref, wo_ref, g_ref, w1_ref, w2_ref, out_ref, u_ref):
    g = g_ref[...]
    h1 = h_ref[...] + _rms(_dot(y_ref[...], wo_ref[...]), g[1:2])
    a = _rms(h1, g[2:3]).astype(BF16)
    for c in range(D_FF // FF_CHUNK):
        cols = slice(c * FF_CHUNK, (c + 1) * FF_CHUNK)
        u = jnp.maximum(_dot(a, w1_ref[:, cols]), 0.0)
        u_ref[:, cols] = (u * u).astype(BF16)
    out_ref[...] = h1 + _rms(_dot(u_ref[...], w2_ref[...]), g[3:4])


def _post_mlp(h, y, w_o, gains, w1, w2):
    t = h.shape[0]
    tm = ROW_TILE
    row = pl.BlockSpec((tm, D_MODEL), lambda i: (i, 0))
    once = lambda shape: pl.BlockSpec(shape, lambda i: (0,) * len(shape), pipeline_mode=pl.Buffered(1))
    return pl.pallas_call(
        _post_mlp_kernel,
        grid=(t // tm,),
        in_specs=[row, row, once(w_o.shape), _const_spec(gains.shape), once(w1.shape), once(w2.shape)],
        out_specs=row,
        out_shape=jax.ShapeDtypeStruct((t, D_MODEL), F32),
        scratch_shapes=[pltpu.VMEM((tm, D_FF), BF16)],
        compiler_params=_params("parallel"),
        name="post_mlp",
    )(h, y, w_o, gains, w1, w2)


def _swap_halves(w):
    half = MLA_ROPE // 2
    g = w.reshape(w.shape[0], -1, 2, half)
    return g[:, :, ::-1, :].reshape(w.shape)


def _mla_weights(w_in, w_uq, w_ukv):
    nq, nkv = MLA_Q_LORA, MLA_KV_LORA
    kr = w_in[:, nq + nkv:]
    krs = _swap_halves(kr)
    w_in_x = jnp.concatenate([w_in[:, :nq + nkv], kr, kr, krs, krs], axis=1).astype(BF16)
    uq = w_uq.reshape(nq, MLA_HEADS, MLA_NOPE + MLA_ROPE)
    uq_nope = uq[:, :, :MLA_NOPE].reshape(nq, -1)
    uq_rope = uq[:, :, MLA_NOPE:].reshape(nq, -1)
    w_uq_x = jnp.concatenate([uq_nope, uq_rope, _swap_halves(uq_rope)], axis=1).astype(BF16)
    ukv = w_ukv.reshape(nkv, MLA_HEADS, MLA_NOPE + MLA_V)
    w_ukv_x = jnp.concatenate([ukv[:, :, :MLA_NOPE].reshape(nkv, -1), ukv[:, :, MLA_NOPE:].reshape(nkv, -1)],
                              axis=1).astype(BF16)
    return w_in_x, w_uq_x, w_ukv_x


def kernel(x, positions, norm_gains, mla_w_in, mla_q_norm, mla_kv_norm, mla_w_uq, mla_w_ukv, mla_w_o, hgrn_w_in,
           hgrn_lb_logits, hgrn_o_norm, hgrn_w_o, mlp_w1, mlp_w2):
    bsz, seq, d = x.shape
    t = bsz * seq

    inv_freq = jnp.power(ROPE_BASE, -jnp.arange(0, MLA_ROPE, 2, dtype=F32) / MLA_ROPE)
    ang = positions.astype(F32).reshape(t, 1) * inv_freq
    cos, sin = jnp.cos(ang), jnp.sin(ang)
    cos_t = jnp.concatenate([cos, cos, cos, cos], axis=1)
    sin_t = jnp.concatenate([-sin, sin, -sin, sin], axis=1)
    p = jax.nn.softmax(hgrn_lb_logits.astype(F32), axis=0)
    lower_bounds = jnp.cumsum(p, axis=0) - p[0]
    q_scale = (MLA_NOPE + MLA_ROPE) ** -0.5 * math.log2(math.e)

    h = x.reshape(t, d)
    for layer in range(DEPTH):
        slot = layer // N_MIXERS
        gains = norm_gains[layer]
        if layer % N_MIXERS == 0:
            w_in, w_uq, w_ukv = _mla_weights(mla_w_in[slot], mla_w_uq[slot], mla_w_ukv[slot])
            qn, qr, kf, vt = _mla_proj(h, gains[0:1], w_in, mla_q_norm[slot][None], mla_kv_norm[slot][None],
                                       w_uq, w_ukv, cos_t, sin_t, q_scale)
            y = _attention(qn.reshape(bsz, seq, -1), qr.reshape(bsz, seq, -1), kf.reshape(bsz, seq, -1),
                           vt.reshape(bsz, seq // ATTN_TILE, -1, ATTN_TILE))
            w_o = mla_w_o[slot]
        else:
            q, f, v, gate = _hgrn_proj(h, gains[0:1], hgrn_w_in[slot].astype(BF16), lower_bounds[layer][None])
            shp = (bsz, seq, -1)
            y = _hgrn_rec(q.reshape(shp), f.reshape(shp), v.reshape(shp), gate.reshape(shp),
                          hgrn_o_norm[slot][None])
            w_o = hgrn_w_o[slot]
        h = _post_mlp(h, y.reshape(t, -1), w_o.astype(BF16), gains, mlp_w1[layer].astype(BF16),
                      mlp_w2[layer].astype(BF16))
    return h.reshape(bsz, seq, d)
```

```python
import functools
import math

import jax
import jax.numpy as jnp
from jax import lax
from jax.experimental import pallas as pl
from jax.experimental.pallas import tpu as pltpu

D_MODEL = 1024
DEPTH = 4
N_MIXERS = 2

MLA_HEADS = 8
MLA_Q_LORA = 512
MLA_KV_LORA = 256
MLA_NOPE = 128
MLA_ROPE = 64
MLA_V = 128
ROPE_BASE = 10000.0

HGRN_HEADS = 8
HGRN_DK = D_MODEL // HGRN_HEADS
HGRN_DV = D_MODEL // HGRN_HEADS
HGRN_CHUNK = 32

D_FF = 4 * D_MODEL
EPS = 1e-6

LANES = 128
VMEM_LIMIT = 56 * 1024 * 1024

ROW_TILE = 512
ATTN_TILE = 256
ATTN_HEADS = 2
HGRN_TILE = 256
HGRN_STEP_HEADS = 2
FF_CHUNK = 1024

F32 = jnp.float32
BF16 = jnp.bfloat16


def _rms(x, g):
    return x * lax.rsqrt(jnp.mean(x * x, axis=-1, keepdims=True) + EPS) * g


def _dot(a, b):
    return jnp.dot(a, b, preferred_element_type=F32)


def _dot_nt(a, b):
    return lax.dot_general(a, b, (((1,), (1,)), ((), ())), preferred_element_type=F32)


def _dot_tn(a, b):
    return lax.dot_general(a, b, (((0,), (0,)), ((), ())), preferred_element_type=F32)


def _const_spec(shape):
    return pl.BlockSpec(shape, lambda *_: (0,) * len(shape))


def _layer_spec(stack, layer):
    return pl.BlockSpec((None,) + stack.shape[1:], lambda *_: (layer,) + (0,) * (stack.ndim - 1),
                        pipeline_mode=pl.Buffered(1))


def _params(*sem):
    return pltpu.CompilerParams(dimension_semantics=sem, vmem_limit_bytes=VMEM_LIMIT)


def _mla_proj_kernel(h_ref, g_ref, win_ref, qg_ref, kvg_ref, wuq_ref, wukv_ref, cos_ref, sin_ref,
                     qn_ref, qr_ref, kf_ref, vt_ref, *, q_scale):
    nq, nkv = MLA_Q_LORA, MLA_KV_LORA
    hn = MLA_HEADS * MLA_NOPE
    a = _rms(h_ref[...], g_ref[...]).astype(BF16)
    proj = _dot(a, win_ref[...])
    c_q = _rms(proj[:, :nq], qg_ref[...]).astype(BF16)
    c_kv = _rms(proj[:, nq:nq + nkv], kvg_ref[...]).astype(BF16)
    cos = cos_ref[...]
    sin = sin_ref[...]
    k_rope = proj[:, nq + nkv:nq + nkv + LANES] * cos + proj[:, nq + nkv + LANES:] * sin
    q = _dot(c_q, wuq_ref[...])
    qn_ref[...] = (q[:, :hn] * q_scale).astype(BF16)
    n_pair = MLA_HEADS * MLA_ROPE // LANES
    for p in range(n_pair):
        r = q[:, hn + p * LANES:hn + (p + 1) * LANES]
        rs = q[:, hn + (n_pair + p) * LANES:hn + (n_pair + p + 1) * LANES]
        qr_ref[:, p * LANES:(p + 1) * LANES] = ((r * cos + rs * sin) * q_scale).astype(BF16)
    kv = _dot(c_kv, wukv_ref[...])
    vt = kv[:, hn:].T.astype(BF16)
    for c in range(vt_ref.shape[0]):
        vt_ref[c] = vt[:, c * ATTN_TILE:(c + 1) * ATTN_TILE]
    lane = lax.broadcasted_iota(jnp.int32, k_rope.shape, 1)
    kr_even = jnp.where(lane < MLA_ROPE, k_rope, 0.0).astype(BF16)
    kr_odd = jnp.where(lane >= MLA_ROPE, k_rope, 0.0).astype(BF16)
    for hd in range(MLA_HEADS):
        off = (hd % 2) * 2 * LANES
        kf_ref[0, hd // 2, :, off:off + LANES] = kv[:, hd * LANES:(hd + 1) * LANES].astype(BF16)
        kf_ref[0, hd // 2, :, off + LANES:off + 2 * LANES] = kr_even if hd % 2 == 0 else kr_odd


def _mla_proj(h, gain, w_in, q_gain, kv_gain, w_uq, w_ukv, cos, sin, q_scale, bsz):
    t = h.shape[0]
    tm = ROW_TILE
    n_s = t // bsz // tm
    row = lambda w: pl.BlockSpec((tm, w), lambda i: (i, 0))
    hn = MLA_HEADS * MLA_NOPE
    return pl.pallas_call(
        functools.partial(_mla_proj_kernel, q_scale=q_scale),
        grid=(t // tm,),
        in_specs=[row(D_MODEL), _const_spec(gain.shape), _const_spec(w_in.shape), _const_spec(q_gain.shape),
                  _const_spec(kv_gain.shape), _const_spec(w_uq.shape), _const_spec(w_ukv.shape),
                  row(LANES), row(LANES)],
        out_specs=[row(hn), row(MLA_HEADS * MLA_ROPE),
                   pl.BlockSpec((1, MLA_HEADS // 2, tm, 4 * LANES), lambda i: (i // n_s, 0, i % n_s, 0)),
                   pl.BlockSpec((tm // ATTN_TILE, hn, ATTN_TILE), lambda i: (i, 0, 0))],
        out_shape=[jax.ShapeDtypeStruct((t, hn), BF16), jax.ShapeDtypeStruct((t, MLA_HEADS * MLA_ROPE), BF16),
                   jax.ShapeDtypeStruct((bsz, MLA_HEADS // 2, t // bsz, 4 * LANES), BF16),
                   jax.ShapeDtypeStruct((t // ATTN_TILE, hn, ATTN_TILE), BF16)],
        compiler_params=_params("parallel"),
        name="mla_proj",
    )(h, gain, w_in, q_gain, kv_gain, w_uq, w_ukv, cos, sin)


def _attn_kernel(qn_ref, qr_ref, k_ref, vt_ref, o_ref, qt_ref, sa_ref, sb_ref, pa_ref, pb_ref, aa_ref, ab_ref,
                 m_ref, l_ref, acc_ref):
    tq = ATTN_TILE
    heads = range(ATTN_HEADS)
    i = pl.program_id(2)
    for hd in heads:
        qt_ref[hd, :LANES, :] = qn_ref[0, :, hd * LANES:(hd + 1) * LANES].T
        qt_ref[hd, LANES:, :] = qr_ref[0, :, (hd // 2) * LANES:(hd // 2 + 1) * LANES].T
    m_ref[...] = jnp.full(m_ref.shape, -jnp.inf, F32)
    l_ref[...] = jnp.zeros(l_ref.shape, F32)
    acc_ref[...] = jnp.zeros(acc_ref.shape, F32)
    pb_ref[...] = jnp.zeros(pb_ref.shape, BF16)
    ab_ref[...] = jnp.ones(ab_ref.shape, F32)

    def scores(j, st_ref):
        start = pl.multiple_of(j * tq, tq)
        for hd in heads:
            st_ref[hd] = _dot(k_ref[0, 0, pl.ds(start, tq), 2 * hd * LANES:(2 * hd + 2) * LANES], qt_ref[hd])

    def softmax(st_ref, pt_ref, alpha_ref, masked=False):
        for hd in heads:
            st = st_ref[hd]
            if masked:
                key = lax.broadcasted_iota(jnp.int32, st.shape, 0)
                qry = lax.broadcasted_iota(jnp.int32, st.shape, 1)
                st = jnp.where(qry >= key, st, -jnp.inf)
            m = m_ref[hd]
            m_new = jnp.maximum(m, jnp.max(st, axis=0, keepdims=True))
            alpha = jnp.exp2(m - m_new)
            pt = jnp.exp2(st - m_new)
            m_ref[hd] = m_new
            l_ref[hd] = alpha * l_ref[hd] + jnp.sum(pt, axis=0, keepdims=True)
            pt_ref[hd] = pt.astype(BF16)
            alpha_ref[hd] = alpha

    def accumulate(j, pt_ref, alpha_ref):
        for hd in heads:
            pv = _dot(vt_ref[0, j, hd * MLA_V:(hd + 1) * MLA_V, :], pt_ref[hd])
            acc_ref[hd] = alpha_ref[hd] * acc_ref[hd] + pv

    def pair(p, carry):
        scores(2 * p + 1, sb_ref)
        accumulate(jnp.maximum(2 * p - 1, 0), pb_ref, ab_ref)
        softmax(sa_ref, pa_ref, aa_ref)
        scores(2 * p + 2, sa_ref)
        accumulate(2 * p, pa_ref, aa_ref)
        softmax(sb_ref, pb_ref, ab_ref)
        return carry

    scores(0, sa_ref)
    lax.fori_loop(0, i // 2, pair, 0)

    @pl.when(i % 2 == 0)
    def _():
        accumulate(jnp.maximum(i - 1, 0), pb_ref, ab_ref)
        softmax(sa_ref, pa_ref, aa_ref, masked=True)
        accumulate(i, pa_ref, aa_ref)

    @pl.when(i % 2 == 1)
    def _():
        scores(i, sb_ref)
        accumulate(jnp.maximum(i - 2, 0), pb_ref, ab_ref)
        softmax(sa_ref, pa_ref, aa_ref)
        accumulate(i - 1, pa_ref, aa_ref)
        softmax(sb_ref, pb_ref, ab_ref, masked=True)
        accumulate(i, pb_ref, ab_ref)

    for hd in heads:
        o_ref[0, :, hd * MLA_V:(hd + 1) * MLA_V] = (acc_ref[hd] / l_ref[hd]).T.astype(o_ref.dtype)


def _attention(qn, qr, kf, vt):
    b, s, _ = qn.shape
    tq = ATTN_TILE
    nh = ATTN_HEADS
    return pl.pallas_call(
        _attn_kernel,
        grid=(b, MLA_HEADS // nh, s // tq),
        in_specs=[pl.BlockSpec((1, tq, nh * LANES), lambda bi, hg, i: (bi, i, hg)),
                  pl.BlockSpec((1, tq, nh * LANES // 2), lambda bi, hg, i: (bi, i, hg)),
                  pl.BlockSpec((1, 1, s, nh * 2 * LANES), lambda bi, hg, i: (bi, hg, 0, 0)),
                  pl.BlockSpec((1, s // tq, nh * MLA_V, tq), lambda bi, hg, i: (bi, 0, hg, 0))],
        out_specs=pl.BlockSpec((1, tq, nh * MLA_V), lambda bi, hg, i: (bi, i, hg)),
        out_shape=jax.ShapeDtypeStruct((b, s, MLA_HEADS * MLA_V), BF16),
        scratch_shapes=[pltpu.VMEM((nh, 2 * LANES, tq), BF16), pltpu.VMEM((nh, tq, tq), F32),
                        pltpu.VMEM((nh, tq, tq), F32), pltpu.VMEM((nh, tq, tq), BF16),
                        pltpu.VMEM((nh, tq, tq), BF16), pltpu.VMEM((nh, 1, tq), F32), pltpu.VMEM((nh, 1, tq), F32),
                        pltpu.VMEM((nh, 1, tq), F32), pltpu.VMEM((nh, 1, tq), F32),
                        pltpu.VMEM((nh, MLA_V, tq), F32)],
        compiler_params=_params("parallel", "parallel", "arbitrary"),
        name="mla_attention",
    )(qn, qr, kf, vt)


def _hgrn_proj_kernel(h_ref, g_ref, win_ref, lb_ref, q_ref, f_ref, v_ref, gate_ref):
    hk = HGRN_HEADS * HGRN_DK
    a = _rms(h_ref[...], g_ref[...]).astype(BF16)
    lb = lb_ref[...]
    qx = _dot(a, win_ref[:, :hk])
    q_ref[...] = qx * jax.nn.sigmoid(qx)
    fx = _dot(a, win_ref[:, hk:2 * hk])
    f_ref[...] = lb + (1.0 - lb) * jax.nn.sigmoid(fx)
    v_ref[...] = _dot(a, win_ref[:, 2 * hk:3 * hk]).astype(BF16)
    gx = _dot(a, win_ref[:, 3 * hk:])
    gate_ref[...] = (gx * jax.nn.sigmoid(gx)).astype(BF16)


def _hgrn_proj(h, gain, w_in, slot, lb):
    t = h.shape[0]
    tm = ROW_TILE
    hk = HGRN_HEADS * HGRN_DK
    row = pl.BlockSpec((tm, hk), lambda i: (i, 0))
    return pl.pallas_call(
        _hgrn_proj_kernel,
        grid=(t // tm,),
        in_specs=[row, _const_spec(gain.shape), _layer_spec(w_in, slot), _const_spec(lb.shape)],
        out_specs=[row, row, row, row],
        out_shape=[jax.ShapeDtypeStruct((t, hk), F32), jax.ShapeDtypeStruct((t, hk), F32),
                   jax.ShapeDtypeStruct((t, hk), BF16), jax.ShapeDtypeStruct((t, hk), BF16)],
        compiler_params=_params("parallel"),
        name="hgrn_proj",
    )(h, gain, w_in, lb)


def _chunk_cumsum(x, pos):
    shift = 1
    while shift < HGRN_CHUNK:
        x = x + jnp.where(pos >= shift, pltpu.roll(x, shift, 0), 0.0)
        shift *= 2
    return x


def _hgrn_rec_kernel(q_ref, f_ref, v_ref, gate_ref, on_ref, y_ref, state_ref):
    c = HGRN_CHUNK
    tl = HGRN_TILE
    nsub = tl // c

    @pl.when(pl.program_id(2) == 0)
    def _():
        state_ref[...] = jnp.zeros_like(state_ref)

    heads = range(HGRN_STEP_HEADS)
    chunks = [slice(n * c, (n + 1) * c) for n in range(nsub)]
    pos = lax.broadcasted_iota(jnp.int32, (tl, HGRN_DK), 0) % c
    row = lax.broadcasted_iota(jnp.int32, (tl, tl), 0)
    col = lax.broadcasted_iota(jnp.int32, (tl, tl), 1)
    keep = (row // c == col // c) & (row >= col)

    vs, q_decs, decays, scores, kvts = [], [], [], [], []
    for hd in heads:
        cols = slice(hd * HGRN_DK, (hd + 1) * HGRN_DK)
        q = q_ref[0, :, cols]
        f = f_ref[0, :, cols]
        v = v_ref[0, :, cols]
        k = 1.0 - f
        b = _chunk_cumsum(jnp.log2(f), pos)

        def per_chunk(fn, b=b):
            return jnp.concatenate([jnp.broadcast_to(fn(b[rows, :]), (c, HGRN_DK)) for rows in chunks], axis=0)

        b_ref = per_chunk(lambda bc: bc[c // 2:c // 2 + 1])
        q_rel = q * jnp.exp2(b - b_ref)
        k_rel = k * jnp.exp2(b_ref - b)
        q_dec = q_rel * per_chunk(lambda bc: jnp.exp2(bc[c // 2:c // 2 + 1]))
        k_dec = (k_rel * per_chunk(lambda bc: jnp.exp2(bc[c - 1:c] - bc[c // 2:c // 2 + 1]))).astype(BF16)
        scores.append(_dot_nt(q_rel.astype(BF16), k_rel.astype(BF16)))
        kvts.append([_dot_tn(v[rows], k_dec[rows]) for rows in chunks])
        q_decs.append(q_dec.astype(BF16))
        decays.append([jnp.exp2(b[rows.stop - 1:rows.stop, :]) for rows in chunks])
        vs.append(v)

    for hd in heads:
        st = state_ref[hd]
        states = []
        for n in range(nsub):
            states.append(st.astype(BF16))
            st = st * decays[hd][n] + kvts[hd][n]
        state_ref[hd] = st
        o = _dot(jnp.where(keep, scores[hd], 0.0).astype(BF16), vs[hd])
        o = o + jnp.concatenate([_dot_nt(q_decs[hd][rows], states[n]) for n, rows in enumerate(chunks)], axis=0)
        cols = slice(hd * HGRN_DV, (hd + 1) * HGRN_DV)
        y = _rms(o, on_ref[...]) * gate_ref[0, :, cols].astype(F32)
        y_ref[0, :, cols] = y.astype(y_ref.dtype)


def _hgrn_rec(q, f, v, gate, o_gain):
    b, s, _ = q.shape
    nh = HGRN_STEP_HEADS
    blk = pl.BlockSpec((1, HGRN_TILE, nh * HGRN_DK), lambda bi, hg, i: (bi, i, hg))
    return pl.pallas_call(
        _hgrn_rec_kernel,
        grid=(b, HGRN_HEADS // nh, s // HGRN_TILE),
        in_specs=[blk, blk, blk, blk, _const_spec(o_gain.shape)],
        out_specs=blk,
        out_shape=jax.ShapeDtypeStruct((b, s, HGRN_HEADS * HGRN_DV), BF16),
        scratch_shapes=[pltpu.VMEM((nh, HGRN_DV, HGRN_DK), F32)],
        compiler_params=_params("parallel", "parallel", "arbitrary"),
        name="hgrn_recurrence",
    )(q, f, v, gate, o_gain)


def _post_mlp_kernel(h_ref, y_ref, wo_ref, g_ref, w1_ref, w2_ref, out_ref, u_ref):
    g = g_ref[...]
    h1 = h_ref[...] + _rms(_dot(y_ref[...], wo_ref[...]), g[1:2])
    a = _rms(h1, g[2:3]).astype(BF16)
    for c in range(D_FF // FF_CHUNK):
        cols = slice(c * FF_CHUNK, (c + 1) * FF_CHUNK)
        u = jnp.maximum(_dot(a, w1_ref[:, cols]), 0.0)
        u_ref[:, cols] = (u * u).astype(BF16)
    out_ref[...] = h1 + _rms(_dot(u_ref[...], w2_ref[...]), g[3:4])


def _post_mlp(h, y, w_o, slot, gains, w1, w2, layer):
    t = h.shape[0]
    tm = ROW_TILE
    row = pl.BlockSpec((tm, D_MODEL), lambda i: (i, 0))
    return pl.pallas_call(
        _post_mlp_kernel,
        grid=(t // tm,),
        in_specs=[row, row, _layer_spec(w_o, slot), _layer_spec(gains, layer), _layer_spec(w1, layer),
                  _layer_spec(w2, layer)],
        out_specs=row,
        out_shape=jax.ShapeDtypeStruct((t, D_MODEL), F32),
        scratch_shapes=[pltpu.VMEM((tm, D_FF), BF16)],
        compiler_params=_params("parallel"),
        name="post_mlp",
    )(h, y, w_o, gains, w1, w2)


def _swap_halves(w):
    half = MLA_ROPE // 2
    g = w.reshape(w.shape[0], -1, 2, half)
    return g[:, :, ::-1, :].reshape(w.shape)


def _mla_weights(w_in, w_uq, w_ukv):
    nq, nkv = MLA_Q_LORA, MLA_KV_LORA
    kr = w_in[:, nq + nkv:]
    krs = _swap_halves(kr)
    w_in_x = jnp.concatenate([w_in[:, :nq + nkv], kr, kr, krs, krs], axis=1).astype(BF16)
    uq = w_uq.reshape(nq, MLA_HEADS, MLA_NOPE + MLA_ROPE)
    uq_nope = uq[:, :, :MLA_NOPE].reshape(nq, -1)
    uq_rope = uq[:, :, MLA_NOPE:].reshape(nq, -1)
    w_uq_x = jnp.concatenate([uq_nope, uq_rope, _swap_halves(uq_rope)], axis=1).astype(BF16)
    ukv = w_ukv.reshape(nkv, MLA_HEADS, MLA_NOPE + MLA_V)
    w_ukv_x = jnp.concatenate([ukv[:, :, :MLA_NOPE].reshape(nkv, -1), ukv[:, :, MLA_NOPE:].reshape(nkv, -1)],
                              axis=1).astype(BF16)
    return w_in_x, w_uq_x, w_ukv_x


def kernel(x, positions, norm_gains, mla_w_in, mla_q_norm, mla_kv_norm, mla_w_uq, mla_w_ukv, mla_w_o, hgrn_w_in,
           hgrn_lb_logits, hgrn_o_norm, hgrn_w_o, mlp_w1, mlp_w2):
    bsz, seq, d = x.shape
    t = bsz * seq

    inv_freq = jnp.power(ROPE_BASE, -jnp.arange(0, MLA_ROPE, 2, dtype=F32) / MLA_ROPE)
    ang = positions.astype(F32).reshape(t, 1) * inv_freq
    cos, sin = jnp.cos(ang), jnp.sin(ang)
    cos_t = jnp.concatenate([cos, cos, cos, cos], axis=1)
    sin_t = jnp.concatenate([-sin, sin, -sin, sin], axis=1)
    p = jax.nn.softmax(hgrn_lb_logits.astype(F32), axis=0)
    lower_bounds = jnp.cumsum(p, axis=0) - p[0]
    q_scale = (MLA_NOPE + MLA_ROPE) ** -0.5 * math.log2(math.e)

    mla_w_o, hgrn_w_in, hgrn_w_o, mlp_w1, mlp_w2 = (w.astype(BF16) for w in (mla_w_o, hgrn_w_in, hgrn_w_o, mlp_w1, mlp_w2))

    h = x.reshape(t, d)
    for layer in range(DEPTH):
        slot = layer // N_MIXERS
        if layer % N_MIXERS == 0:
            w_in, w_uq, w_ukv = _mla_weights(mla_w_in[slot], mla_w_uq[slot], mla_w_ukv[slot])
            qn, qr, kf, vt = _mla_proj(h, norm_gains[layer, 0:1], w_in, mla_q_norm[slot][None],
                                       mla_kv_norm[slot][None], w_uq, w_ukv, cos_t, sin_t, q_scale, bsz)
            y = _attention(qn.reshape(bsz, seq, -1), qr.reshape(bsz, seq, -1), kf,
                           vt.reshape(bsz, seq // ATTN_TILE, -1, ATTN_TILE))
            w_o = mla_w_o
        else:
            q, f, v, gate = _hgrn_proj(h, norm_gains[layer, 0:1], hgrn_w_in, slot, lower_bounds[layer][None])
            shp = (bsz, seq, -1)
            y = _hgrn_rec(q.reshape(shp), f.reshape(shp), v.reshape(shp), gate.reshape(shp),
                          hgrn_o_norm[slot][None])
            w_o = hgrn_w_o
        h = _post_mlp(h, y.reshape(t, -1), w_o, slot, norm_gains, mlp_w1, mlp_w2, layer)
    return h.reshape(bsz, seq, d)
```

```python
import functools
import math

import jax
import jax.numpy as jnp
from jax import lax
from jax.experimental import pallas as pl
from jax.experimental.pallas import tpu as pltpu

D_MODEL = 1024
DEPTH = 4
N_MIXERS = 2

MLA_HEADS = 8
MLA_Q_LORA = 512
MLA_KV_LORA = 256
MLA_NOPE = 128
MLA_ROPE = 64
MLA_V = 128
ROPE_BASE = 10000.0

HGRN_HEADS = 8
HGRN_DK = D_MODEL // HGRN_HEADS
HGRN_DV = D_MODEL // HGRN_HEADS
HGRN_CHUNK = 32

D_FF = 4 * D_MODEL
EPS = 1e-6

LANES = 128
VMEM_LIMIT = 56 * 1024 * 1024

ROW_TILE = 512
ATTN_TILE = 256
ATTN_HEADS = 2
ATTN_UNROLL = 4
VT_ROWS = MLA_V + 16
HGRN_TILE = 256
HGRN_STEP_HEADS = 2
FF_CHUNK = 1024

F32 = jnp.float32
BF16 = jnp.bfloat16


def _rms(x, g):
    return x * lax.rsqrt(jnp.mean(x * x, axis=-1, keepdims=True) + EPS) * g


def _dot(a, b):
    return jnp.dot(a, b, preferred_element_type=F32)


def _dot_nt(a, b):
    return lax.dot_general(a, b, (((1,), (1,)), ((), ())), preferred_element_type=F32)


def _dot_tn(a, b):
    return lax.dot_general(a, b, (((0,), (0,)), ((), ())), preferred_element_type=F32)


def _const_spec(shape):
    return pl.BlockSpec(shape, lambda *_: (0,) * len(shape))


def _layer_spec(stack, layer):
    return pl.BlockSpec((None,) + stack.shape[1:], lambda *_: (layer,) + (0,) * (stack.ndim - 1),
                        pipeline_mode=pl.Buffered(1))


def _params(*sem):
    return pltpu.CompilerParams(dimension_semantics=sem, vmem_limit_bytes=VMEM_LIMIT)


def _mla_proj_kernel(h_ref, g_ref, win_ref, qg_ref, kvg_ref, wuq_ref, wukv_ref, cos_ref, sin_ref,
                     qn_ref, qr_ref, kf_ref, vt_ref, *, q_scale):
    nq, nkv = MLA_Q_LORA, MLA_KV_LORA
    hn = MLA_HEADS * MLA_NOPE
    a = _rms(h_ref[...], g_ref[...]).astype(BF16)
    proj = _dot(a, win_ref[...])
    c_q = _rms(proj[:, :nq], qg_ref[...]).astype(BF16)
    c_kv = _rms(proj[:, nq:nq + nkv], kvg_ref[...]).astype(BF16)
    cos = cos_ref[...]
    sin = sin_ref[...]
    k_rope = proj[:, nq + nkv:nq + nkv + LANES] * cos + proj[:, nq + nkv + LANES:] * sin
    q = _dot(c_q, wuq_ref[...])
    qn_ref[...] = (q[:, :hn] * q_scale).astype(BF16)
    n_pair = MLA_HEADS * MLA_ROPE // LANES
    for p in range(n_pair):
        r = q[:, hn + p * LANES:hn + (p + 1) * LANES]
        rs = q[:, hn + (n_pair + p) * LANES:hn + (n_pair + p + 1) * LANES]
        qr_ref[:, p * LANES:(p + 1) * LANES] = ((r * cos + rs * sin) * q_scale).astype(BF16)
    kv = _dot(c_kv, wukv_ref[...])
    vt = kv[:, hn:].T.astype(BF16)
    ones = jnp.ones((VT_ROWS - MLA_V, ATTN_TILE), BF16)
    for c in range(vt_ref.shape[0]):
        for hd in range(MLA_HEADS):
            vt_ref[c, hd * VT_ROWS:hd * VT_ROWS + MLA_V, :] = vt[hd * MLA_V:(hd + 1) * MLA_V,
                                                                 c * ATTN_TILE:(c + 1) * ATTN_TILE]
            vt_ref[c, hd * VT_ROWS + MLA_V:(hd + 1) * VT_ROWS, :] = ones
    lane = lax.broadcasted_iota(jnp.int32, k_rope.shape, 1)
    kr_even = jnp.where(lane < MLA_ROPE, k_rope, 0.0).astype(BF16)
    kr_odd = jnp.where(lane >= MLA_ROPE, k_rope, 0.0).astype(BF16)
    for hd in range(MLA_HEADS):
        off = (hd % 2) * 2 * LANES
        kf_ref[0, hd // 2, :, off:off + LANES] = kv[:, hd * LANES:(hd + 1) * LANES].astype(BF16)
        kf_ref[0, hd // 2, :, off + LANES:off + 2 * LANES] = kr_even if hd % 2 == 0 else kr_odd


def _mla_proj(h, gain, w_in, q_gain, kv_gain, w_uq, w_ukv, cos, sin, q_scale, bsz):
    t = h.shape[0]
    tm = ROW_TILE
    n_s = t // bsz // tm
    row = lambda w: pl.BlockSpec((tm, w), lambda i: (i, 0))
    hn = MLA_HEADS * MLA_NOPE
    return pl.pallas_call(
        functools.partial(_mla_proj_kernel, q_scale=q_scale),
        grid=(t // tm,),
        in_specs=[row(D_MODEL), _const_spec(gain.shape), _const_spec(w_in.shape), _const_spec(q_gain.shape),
                  _const_spec(kv_gain.shape), _const_spec(w_uq.shape), _const_spec(w_ukv.shape),
                  row(LANES), row(LANES)],
        out_specs=[row(hn), row(MLA_HEADS * MLA_ROPE),
                   pl.BlockSpec((1, MLA_HEADS // 2, tm, 4 * LANES), lambda i: (i // n_s, 0, i % n_s, 0)),
                   pl.BlockSpec((tm // ATTN_TILE, MLA_HEADS * VT_ROWS, ATTN_TILE), lambda i: (i, 0, 0))],
        out_shape=[jax.ShapeDtypeStruct((t, hn), BF16), jax.ShapeDtypeStruct((t, MLA_HEADS * MLA_ROPE), BF16),
                   jax.ShapeDtypeStruct((bsz, MLA_HEADS // 2, t // bsz, 4 * LANES), BF16),
                   jax.ShapeDtypeStruct((t // ATTN_TILE, MLA_HEADS * VT_ROWS, ATTN_TILE), BF16)],
        compiler_params=_params("parallel"),
        name="mla_proj",
    )(h, gain, w_in, q_gain, kv_gain, w_uq, w_ukv, cos, sin)


def _attn_kernel(qn_ref, qr_ref, k_ref, vt_ref, o_ref, qt_ref, sa_ref, sb_ref, pa_ref, pb_ref, aa_ref, ab_ref,
                 m_ref, l_ref, acc_ref):
    tq = ATTN_TILE
    heads = range(ATTN_HEADS)

    def scores(j, st_ref):
        start = pl.multiple_of(j * tq, tq)
        for hd in heads:
            kc = (hd % 2) * 2 * LANES
            st_ref[hd] = _dot(k_ref[0, hd // 2, pl.ds(start, tq), kc:kc + 2 * LANES], qt_ref[hd])

    def softmax(st_ref, pt_ref, alpha_ref, masked=False):
        for hd in heads:
            st = st_ref[hd]
            if masked:
                key = lax.broadcasted_iota(jnp.int32, st.shape, 0)
                qry = lax.broadcasted_iota(jnp.int32, st.shape, 1)
                st = jnp.where(qry >= key, st, -jnp.inf)
            m = m_ref[hd]
            m_new = jnp.maximum(m, jnp.max(st, axis=0, keepdims=True))
            alpha = jnp.exp2(m - m_new)
            pt = jnp.exp2(st - m_new)
            m_ref[hd] = m_new
            pt_ref[hd] = pt.astype(BF16)
            alpha_ref[hd] = alpha

    def accumulate(j, pt_ref, alpha_ref):
        for hd in heads:
            pv = _dot(vt_ref[0, j, hd * VT_ROWS:(hd + 1) * VT_ROWS, :], pt_ref[hd])
            acc_ref[hd] = alpha_ref[hd] * acc_ref[hd] + pv[:MLA_V]
            l_ref[hd] = alpha_ref[hd] * l_ref[hd] + pv[MLA_V:MLA_V + 1]

    bufs = ((sa_ref, pa_ref, aa_ref), (sb_ref, pb_ref, ab_ref))

    def run(j0, n, last):
        for idx in range(n):
            (s_cur, p_cur, a_cur), (s_nxt, p_prv, a_prv) = bufs[idx % 2], bufs[1 - idx % 2]
            if not (last and idx == n - 1):
                scores(j0 + idx + 1, s_nxt)
            accumulate(jnp.maximum(j0 + idx - 1, 0), p_prv, a_prv)
            softmax(s_cur, p_cur, a_cur, masked=last and idx == n - 1)
        if last:
            accumulate(j0 + n - 1, p_cur, a_cur)

    def group(g, carry):
        run(ATTN_UNROLL * g, ATTN_UNROLL, last=False)
        return carry

    def tile(i, carry):
        rows = pl.ds(pl.multiple_of(i * tq, tq), tq)
        for hd in heads:
            qt_ref[hd, :LANES, :] = qn_ref[0, rows, hd * LANES:(hd + 1) * LANES].T
            qt_ref[hd, LANES:, :] = qr_ref[0, rows, (hd // 2) * LANES:(hd // 2 + 1) * LANES].T
        m_ref[...] = jnp.full(m_ref.shape, -jnp.inf, F32)
        l_ref[...] = jnp.zeros(l_ref.shape, F32)
        acc_ref[...] = jnp.zeros(acc_ref.shape, F32)
        pb_ref[...] = jnp.zeros(pb_ref.shape, BF16)
        ab_ref[...] = jnp.ones(ab_ref.shape, F32)

        scores(0, sa_ref)
        lax.fori_loop(0, i // ATTN_UNROLL, group, 0)
        for rest in range(ATTN_UNROLL):
            @pl.when(i % ATTN_UNROLL == rest)
            def _(rest=rest):
                run(i - rest, rest + 1, last=True)

        for hd in heads:
            o_ref[0, rows, hd * MLA_V:(hd + 1) * MLA_V] = (acc_ref[hd] / l_ref[hd]).T.astype(o_ref.dtype)
        return carry

    lax.fori_loop(0, qn_ref.shape[1] // tq, tile, 0)


def _attention(qn, qr, kf, vt):
    b, s, _ = qn.shape
    tq = ATTN_TILE
    nh = ATTN_HEADS
    return pl.pallas_call(
        _attn_kernel,
        grid=(b, MLA_HEADS // nh),
        in_specs=[pl.BlockSpec((1, s, nh * LANES), lambda bi, hg: (bi, 0, hg)),
                  pl.BlockSpec((1, s, nh * LANES // 2), lambda bi, hg: (bi, 0, hg)),
                  pl.BlockSpec((1, nh // 2, s, 4 * LANES), lambda bi, hg: (bi, hg, 0, 0)),
                  pl.BlockSpec((1, s // tq, nh * VT_ROWS, tq), lambda bi, hg: (bi, 0, hg, 0))],
        out_specs=pl.BlockSpec((1, s, nh * MLA_V), lambda bi, hg: (bi, 0, hg)),
        out_shape=jax.ShapeDtypeStruct((b, s, MLA_HEADS * MLA_V), BF16),
        scratch_shapes=[pltpu.VMEM((nh, 2 * LANES, tq), BF16), pltpu.VMEM((nh, tq, tq), F32),
                        pltpu.VMEM((nh, tq, tq), F32), pltpu.VMEM((nh, tq, tq), BF16),
                        pltpu.VMEM((nh, tq, tq), BF16), pltpu.VMEM((nh, 1, tq), F32), pltpu.VMEM((nh, 1, tq), F32),
                        pltpu.VMEM((nh, 1, tq), F32), pltpu.VMEM((nh, 1, tq), F32),
                        pltpu.VMEM((nh, MLA_V, tq), F32)],
        compiler_params=_params("parallel", "parallel"),
        name="mla_attention",
    )(qn, qr, kf, vt)


def _hgrn_proj_kernel(h_ref, g_ref, win_ref, lb_ref, q_ref, f_ref, v_ref, gate_ref):
    hk = HGRN_HEADS * HGRN_DK
    a = _rms(h_ref[...], g_ref[...]).astype(BF16)
    lb = lb_ref[...]
    qx = _dot(a, win_ref[:, :hk])
    q_ref[...] = qx * jax.nn.sigmoid(qx)
    fx = _dot(a, win_ref[:, hk:2 * hk])
    f_ref[...] = lb + (1.0 - lb) * jax.nn.sigmoid(fx)
    v_ref[...] = _dot(a, win_ref[:, 2 * hk:3 * hk]).astype(BF16)
    gx = _dot(a, win_ref[:, 3 * hk:])
    gate_ref[...] = (gx * jax.nn.sigmoid(gx)).astype(BF16)


def _hgrn_proj(h, gain, w_in, slot, lb):
    t = h.shape[0]
    tm = ROW_TILE
    hk = HGRN_HEADS * HGRN_DK
    row = pl.BlockSpec((tm, hk), lambda i: (i, 0))
    return pl.pallas_call(
        _hgrn_proj_kernel,
        grid=(t // tm,),
        in_specs=[row, _const_spec(gain.shape), _layer_spec(w_in, slot), _const_spec(lb.shape)],
        out_specs=[row, row, row, row],
        out_shape=[jax.ShapeDtypeStruct((t, hk), F32), jax.ShapeDtypeStruct((t, hk), F32),
                   jax.ShapeDtypeStruct((t, hk), BF16), jax.ShapeDtypeStruct((t, hk), BF16)],
        compiler_params=_params("parallel"),
        name="hgrn_proj",
    )(h, gain, w_in, lb)


def _chunk_cumsum(x, pos):
    shift = 1
    while shift < HGRN_CHUNK:
        x = x + jnp.where(pos >= shift, pltpu.roll(x, shift, 0), 0.0)
        shift *= 2
    return x


def _hgrn_rec_kernel(q_ref, f_ref, v_ref, gate_ref, on_ref, y_ref, state_ref):
    c = HGRN_CHUNK
    tl = HGRN_TILE
    nsub = tl // c

    @pl.when(pl.program_id(2) == 0)
    def _():
        state_ref[...] = jnp.zeros_like(state_ref)

    heads = range(HGRN_STEP_HEADS)
    chunks = [slice(n * c, (n + 1) * c) for n in range(nsub)]
    pos = lax.broadcasted_iota(jnp.int32, (tl, HGRN_DK), 0) % c
    row = lax.broadcasted_iota(jnp.int32, (tl, tl), 0)
    col = lax.broadcasted_iota(jnp.int32, (tl, tl), 1)
    keep = (row // c == col // c) & (row >= col)

    vs, q_decs, decays, scores, kvts = [], [], [], [], []
    for hd in heads:
        cols = slice(hd * HGRN_DK, (hd + 1) * HGRN_DK)
        q = q_ref[0, :, cols]
        f = f_ref[0, :, cols]
        v = v_ref[0, :, cols]
        k = 1.0 - f
        b = _chunk_cumsum(jnp.log2(f), pos)

        def per_chunk(fn, b=b):
            return jnp.concatenate([jnp.broadcast_to(fn(b[rows, :]), (c, HGRN_DK)) for rows in chunks], axis=0)

        b_ref = per_chunk(lambda bc: bc[c // 2:c // 2 + 1])
        q_rel = q * jnp.exp2(b - b_ref)
        k_rel = k * jnp.exp2(b_ref - b)
        q_dec = q_rel * per_chunk(lambda bc: jnp.exp2(bc[c // 2:c // 2 + 1]))
        k_dec = (k_rel * per_chunk(lambda bc: jnp.exp2(bc[c - 1:c] - bc[c // 2:c // 2 + 1]))).astype(BF16)
        scores.append(_dot_nt(q_rel.astype(BF16), k_rel.astype(BF16)))
        kvts.append([_dot_tn(v[rows], k_dec[rows]) for rows in chunks])
        q_decs.append(q_dec.astype(BF16))
        decays.append([jnp.exp2(b[rows.stop - 1:rows.stop, :]) for rows in chunks])
        vs.append(v)

    for hd in heads:
        st = state_ref[hd]
        states = []
        for n in range(nsub):
            states.append(st.astype(BF16))
            st = st * decays[hd][n] + kvts[hd][n]
        state_ref[hd] = st
        o = _dot(jnp.where(keep, scores[hd], 0.0).astype(BF16), vs[hd])
        o = o + jnp.concatenate([_dot_nt(q_decs[hd][rows], states[n]) for n, rows in enumerate(chunks)], axis=0)
        cols = slice(hd * HGRN_DV, (hd + 1) * HGRN_DV)
        y = _rms(o, on_ref[...]) * gate_ref[0, :, cols].astype(F32)
        y_ref[0, :, cols] = y.astype(y_ref.dtype)


def _hgrn_rec(q, f, v, gate, o_gain):
    b, s, _ = q.shape
    nh = HGRN_STEP_HEADS
    blk = pl.BlockSpec((1, HGRN_TILE, nh * HGRN_DK), lambda bi, hg, i: (bi, i, hg))
    return pl.pallas_call(
        _hgrn_rec_kernel,
        grid=(b, HGRN_HEADS // nh, s // HGRN_TILE),
        in_specs=[blk, blk, blk, blk, _const_spec(o_gain.shape)],
        out_specs=blk,
        out_shape=jax.ShapeDtypeStruct((b, s, HGRN_HEADS * HGRN_DV), BF16),
        scratch_shapes=[pltpu.VMEM((nh, HGRN_DV, HGRN_DK), F32)],
        compiler_params=_params("parallel", "parallel", "arbitrary"),
        name="hgrn_recurrence",
    )(q, f, v, gate, o_gain)


def _post_mlp_kernel(h_ref, y_ref, wo_ref, g_ref, w1_ref, w2_ref, out_ref, u_ref):
    g = g_ref[...]
    h1 = h_ref[...] + _rms(_dot(y_ref[...], wo_ref[...]), g[1:2])
    a = _rms(h1, g[2:3]).astype(BF16)
    for c in range(D_FF // FF_CHUNK):
        cols = slice(c * FF_CHUNK, (c + 1) * FF_CHUNK)
        u = jnp.maximum(_dot(a, w1_ref[:, cols]), 0.0)
        u_ref[:, cols] = (u * u).astype(BF16)
    out_ref[...] = h1 + _rms(_dot(u_ref[...], w2_ref[...]), g[3:4])


def _post_mlp(h, y, w_o, slot, gains, w1, w2, layer):
    t = h.shape[0]
    tm = ROW_TILE
    row = pl.BlockSpec((tm, D_MODEL), lambda i: (i, 0))
    return pl.pallas_call(
        _post_mlp_kernel,
        grid=(t // tm,),
        in_specs=[row, row, _layer_spec(w_o, slot), _layer_spec(gains, layer), _layer_spec(w1, layer),
                  _layer_spec(w2, layer)],
        out_specs=row,
        out_shape=jax.ShapeDtypeStruct((t, D_MODEL), F32),
        scratch_shapes=[pltpu.VMEM((tm, D_FF), BF16)],
        compiler_params=_params("parallel"),
        name="post_mlp",
    )(h, y, w_o, gains, w1, w2)


def _swap_halves(w):
    half = MLA_ROPE // 2
    g = w.reshape(w.shape[0], -1, 2, half)
    return g[:, :, ::-1, :].reshape(w.shape)


def _mla_weights(w_in, w_uq, w_ukv):
    nq, nkv = MLA_Q_LORA, MLA_KV_LORA
    kr = w_in[:, nq + nkv:]
    krs = _swap_halves(kr)
    w_in_x = jnp.concatenate([w_in[:, :nq + nkv], kr, kr, krs, krs], axis=1).astype(BF16)
    uq = w_uq.reshape(nq, MLA_HEADS, MLA_NOPE + MLA_ROPE)
    uq_nope = uq[:, :, :MLA_NOPE].reshape(nq, -1)
    uq_rope = uq[:, :, MLA_NOPE:].reshape(nq, -1)
    w_uq_x = jnp.concatenate([uq_nope, uq_rope, _swap_halves(uq_rope)], axis=1).astype(BF16)
    ukv = w_ukv.reshape(nkv, MLA_HEADS, MLA_NOPE + MLA_V)
    w_ukv_x = jnp.concatenate([ukv[:, :, :MLA_NOPE].reshape(nkv, -1), ukv[:, :, MLA_NOPE:].reshape(nkv, -1)],
                              axis=1).astype(BF16)
    return w_in_x, w_uq_x, w_ukv_x


def kernel(x, positions, norm_gains, mla_w_in, mla_q_norm, mla_kv_norm, mla_w_uq, mla_w_ukv, mla_w_o, hgrn_w_in,
           hgrn_lb_logits, hgrn_o_norm, hgrn_w_o, mlp_w1, mlp_w2):
    bsz, seq, d = x.shape
    t = bsz * seq

    inv_freq = jnp.power(ROPE_BASE, -jnp.arange(0, MLA_ROPE, 2, dtype=F32) / MLA_ROPE)
    ang = positions.astype(F32).reshape(t, 1) * inv_freq
    cos, sin = lax.optimization_barrier((jnp.cos(ang), jnp.sin(ang)))
    cos_t = jnp.concatenate([cos, cos, cos, cos], axis=1)
    sin_t = jnp.concatenate([-sin, sin, -sin, sin], axis=1)
    p = jax.nn.softmax(hgrn_lb_logits.astype(F32), axis=0)
    lower_bounds = jnp.cumsum(p, axis=0) - p[0]
    q_scale = (MLA_NOPE + MLA_ROPE) ** -0.5 * math.log2(math.e)

    mla_w_o, hgrn_w_in, hgrn_w_o, mlp_w1, mlp_w2 = (w.astype(BF16) for w in (mla_w_o, hgrn_w_in, hgrn_w_o, mlp_w1, mlp_w2))

    h = x.reshape(t, d)
    for layer in range(DEPTH):
        slot = layer // N_MIXERS
        if layer % N_MIXERS == 0:
            w_in, w_uq, w_ukv = _mla_weights(mla_w_in[slot], mla_w_uq[slot], mla_w_ukv[slot])
            qn, qr, kf, vt = _mla_proj(h, norm_gains[layer, 0:1], w_in, mla_q_norm[slot][None],
                                       mla_kv_norm[slot][None], w_uq, w_ukv, cos_t, sin_t, q_scale, bsz)
            y = _attention(qn.reshape(bsz, seq, -1), qr.reshape(bsz, seq, -1), kf,
                           vt.reshape(bsz, seq // ATTN_TILE, -1, ATTN_TILE))
            w_o = mla_w_o
        else:
            q, f, v, gate = _hgrn_proj(h, norm_gains[layer, 0:1], hgrn_w_in, slot, lower_bounds[layer][None])
            shp = (bsz, seq, -1)
            y = _hgrn_rec(q.reshape(shp), f.reshape(shp), v.reshape(shp), gate.reshape(shp),
                          hgrn_o_norm[slot][None])
            w_o = hgrn_w_o
        h = _post_mlp(h, y.reshape(t, -1), w_o, slot, norm_gains, mlp_w1, mlp_w2, layer)
    return h.reshape(bsz, seq, d)
```

```python
import functools
import math

import jax
import jax.numpy as jnp
from jax import lax
from jax.experimental import pallas as pl
from jax.experimental.pallas import tpu as pltpu

D_MODEL = 1024
DEPTH = 4
N_MIXERS = 2

MLA_HEADS = 8
MLA_Q_LORA = 512
MLA_KV_LORA = 256
MLA_NOPE = 128
MLA_ROPE = 64
MLA_V = 128
ROPE_BASE = 10000.0

HGRN_HEADS = 8
HGRN_DK = D_MODEL // HGRN_HEADS
HGRN_DV = D_MODEL // HGRN_HEADS
HGRN_CHUNK = 32

D_FF = 4 * D_MODEL
EPS = 1e-6

LANES = 128
VMEM_LIMIT = 56 * 1024 * 1024

ROW_TILE = 512
ATTN_TQ = 512
ATTN_TK = 256
ATTN_HEADS = 2
ATTN_UNROLL = 4
VT_ROWS = MLA_V + 16
HGRN_TILE = 256
HGRN_STEP_HEADS = 4
FF_CHUNK = 1024

F32 = jnp.float32
BF16 = jnp.bfloat16


def _rms(x, g):
    return x * lax.rsqrt(jnp.mean(x * x, axis=-1, keepdims=True) + EPS) * g


def _dot(a, b):
    return jnp.dot(a, b, preferred_element_type=F32)


def _dot_nt(a, b):
    return lax.dot_general(a, b, (((1,), (1,)), ((), ())), preferred_element_type=F32)


def _dot_tn(a, b):
    return lax.dot_general(a, b, (((0,), (0,)), ((), ())), preferred_element_type=F32)


def _const_spec(shape):
    return pl.BlockSpec(shape, lambda *_: (0,) * len(shape))


def _layer_spec(stack, layer):
    return pl.BlockSpec((None,) + stack.shape[1:], lambda *_: (layer,) + (0,) * (stack.ndim - 1),
                        pipeline_mode=pl.Buffered(1))


def _params(*sem):
    return pltpu.CompilerParams(dimension_semantics=sem, vmem_limit_bytes=VMEM_LIMIT)


def _mla_proj_kernel(h_ref, g_ref, win_ref, qg_ref, kvg_ref, wuq_ref, wukv_ref, cos_ref, sin_ref,
                     qn_ref, qr_ref, kf_ref, vt_ref, *, q_scale):
    nq, nkv = MLA_Q_LORA, MLA_KV_LORA
    hn = MLA_HEADS * MLA_NOPE
    a = _rms(h_ref[...], g_ref[...]).astype(BF16)
    proj = _dot(a, win_ref[...])
    c_q = _rms(proj[:, :nq], qg_ref[...]).astype(BF16)
    c_kv = _rms(proj[:, nq:nq + nkv], kvg_ref[...]).astype(BF16)
    cos = cos_ref[...]
    sin = sin_ref[...]
    k_rope = proj[:, nq + nkv:nq + nkv + LANES] * cos + proj[:, nq + nkv + LANES:] * sin
    q = _dot(c_q, wuq_ref[...])
    qn_ref[...] = (q[:, :hn] * q_scale).astype(BF16)
    n_pair = MLA_HEADS * MLA_ROPE // LANES
    for p in range(n_pair):
        r = q[:, hn + p * LANES:hn + (p + 1) * LANES]
        rs = q[:, hn + (n_pair + p) * LANES:hn + (n_pair + p + 1) * LANES]
        qr_ref[:, p * LANES:(p + 1) * LANES] = ((r * cos + rs * sin) * q_scale).astype(BF16)
    kv = _dot(c_kv, wukv_ref[...])
    vt = kv[:, hn:].T.astype(BF16)
    ones = jnp.ones((VT_ROWS - MLA_V, ATTN_TK), BF16)
    for c in range(vt_ref.shape[0]):
        for hd in range(MLA_HEADS):
            vt_ref[c, hd * VT_ROWS:hd * VT_ROWS + MLA_V, :] = vt[hd * MLA_V:(hd + 1) * MLA_V,
                                                                 c * ATTN_TK:(c + 1) * ATTN_TK]
            vt_ref[c, hd * VT_ROWS + MLA_V:(hd + 1) * VT_ROWS, :] = ones
    lane = lax.broadcasted_iota(jnp.int32, k_rope.shape, 1)
    kr_even = jnp.where(lane < MLA_ROPE, k_rope, 0.0).astype(BF16)
    kr_odd = jnp.where(lane >= MLA_ROPE, k_rope, 0.0).astype(BF16)
    for hd in range(MLA_HEADS):
        off = (hd % 2) * 2 * LANES
        kf_ref[0, hd // 2, :, off:off + LANES] = kv[:, hd * LANES:(hd + 1) * LANES].astype(BF16)
        kf_ref[0, hd // 2, :, off + LANES:off + 2 * LANES] = kr_even if hd % 2 == 0 else kr_odd


def _mla_proj(h, gain, w_in, q_gain, kv_gain, w_uq, w_ukv, cos, sin, q_scale, bsz):
    t = h.shape[0]
    tm = ROW_TILE
    n_s = t // bsz // tm
    row = lambda w: pl.BlockSpec((tm, w), lambda i: (i, 0))
    hn = MLA_HEADS * MLA_NOPE
    return pl.pallas_call(
        functools.partial(_mla_proj_kernel, q_scale=q_scale),
        grid=(t // tm,),
        in_specs=[row(D_MODEL), _const_spec(gain.shape), _const_spec(w_in.shape), _const_spec(q_gain.shape),
                  _const_spec(kv_gain.shape), _const_spec(w_uq.shape), _const_spec(w_ukv.shape),
                  row(LANES), row(LANES)],
        out_specs=[row(hn), row(MLA_HEADS * MLA_ROPE),
                   pl.BlockSpec((1, MLA_HEADS // 2, tm, 4 * LANES), lambda i: (i // n_s, 0, i % n_s, 0)),
                   pl.BlockSpec((tm // ATTN_TK, MLA_HEADS * VT_ROWS, ATTN_TK), lambda i: (i, 0, 0))],
        out_shape=[jax.ShapeDtypeStruct((t, hn), BF16), jax.ShapeDtypeStruct((t, MLA_HEADS * MLA_ROPE), BF16),
                   jax.ShapeDtypeStruct((bsz, MLA_HEADS // 2, t // bsz, 4 * LANES), BF16),
                   jax.ShapeDtypeStruct((t // ATTN_TK, MLA_HEADS * VT_ROWS, ATTN_TK), BF16)],
        compiler_params=_params("parallel"),
        name="mla_proj",
    )(h, gain, w_in, q_gain, kv_gain, w_uq, w_ukv, cos, sin)


def _attn_kernel(qn_ref, qr_ref, k_ref, vt_ref, o_ref, qt_ref, sa_ref, sb_ref, pa_ref, pb_ref, aa_ref, ab_ref,
                 m_ref, l_ref, acc_ref):
    tq, tk = ATTN_TQ, ATTN_TK
    diag = tq // tk
    heads = range(ATTN_HEADS)

    def scores(j, st_ref):
        start = pl.multiple_of(j * tk, tk)
        for hd in heads:
            kc = (hd % 2) * 2 * LANES
            st_ref[hd] = _dot(k_ref[0, hd // 2, pl.ds(start, tk), kc:kc + 2 * LANES], qt_ref[hd])

    def softmax(st_ref, pt_ref, alpha_ref, mask_from=None):
        for hd in heads:
            st = st_ref[hd]
            if mask_from is not None:
                key = lax.broadcasted_iota(jnp.int32, st.shape, 0) + mask_from
                qry = lax.broadcasted_iota(jnp.int32, st.shape, 1)
                st = jnp.where(qry >= key, st, -jnp.inf)
            m = m_ref[hd]
            m_new = jnp.maximum(m, jnp.max(st, axis=0, keepdims=True))
            alpha = jnp.exp2(m - m_new)
            pt = jnp.exp2(st - m_new)
            m_ref[hd] = m_new
            pt_ref[hd] = pt.astype(BF16)
            alpha_ref[hd] = alpha

    def accumulate(j, pt_ref, alpha_ref):
        for hd in heads:
            pv = _dot(vt_ref[0, j, hd * VT_ROWS:(hd + 1) * VT_ROWS, :], pt_ref[hd])
            acc_ref[hd] = alpha_ref[hd] * acc_ref[hd] + pv[:MLA_V]
            l_ref[hd] = alpha_ref[hd] * l_ref[hd] + pv[MLA_V:MLA_V + 1]

    bufs = ((sa_ref, pa_ref, aa_ref), (sb_ref, pb_ref, ab_ref))

    def run(j0, n, last):
        for idx in range(n):
            (s_cur, p_cur, a_cur), (s_nxt, p_prv, a_prv) = bufs[idx % 2], bufs[1 - idx % 2]
            if not (last and idx == n - 1):
                scores(j0 + idx + 1, s_nxt)
            accumulate(jnp.maximum(j0 + idx - 1, 0), p_prv, a_prv)
            on_diag = last and idx >= n - diag
            softmax(s_cur, p_cur, a_cur, mask_from=(idx - (n - diag)) * tk if on_diag else None)
        if last:
            accumulate(j0 + n - 1, p_cur, a_cur)

    def group(g, carry):
        run(ATTN_UNROLL * g, ATTN_UNROLL, last=False)
        return carry

    def tile(i, carry):
        rows = pl.ds(pl.multiple_of(i * tq, tq), tq)
        for hd in heads:
            qt_ref[hd, :LANES, :] = qn_ref[0, rows, hd * LANES:(hd + 1) * LANES].T
            qt_ref[hd, LANES:, :] = qr_ref[0, rows, (hd // 2) * LANES:(hd // 2 + 1) * LANES].T
        m_ref[...] = jnp.full(m_ref.shape, -jnp.inf, F32)
        l_ref[...] = jnp.zeros(l_ref.shape, F32)
        acc_ref[...] = jnp.zeros(acc_ref.shape, F32)
        pb_ref[...] = jnp.zeros(pb_ref.shape, BF16)
        ab_ref[...] = jnp.ones(ab_ref.shape, F32)

        below = i * diag
        scores(0, sa_ref)
        lax.fori_loop(0, below // ATTN_UNROLL, group, 0)
        for rest in range(0, ATTN_UNROLL, diag):
            @pl.when(below % ATTN_UNROLL == rest)
            def _(rest=rest):
                run(below - rest, rest + diag, last=True)

        for hd in heads:
            o_ref[0, rows, hd * MLA_V:(hd + 1) * MLA_V] = (acc_ref[hd] / l_ref[hd]).T.astype(o_ref.dtype)
        return carry

    lax.fori_loop(0, qn_ref.shape[1] // tq, tile, 0)


def _attention(qn, qr, kf, vt):
    b, s, _ = qn.shape
    tq, tk = ATTN_TQ, ATTN_TK
    nh = ATTN_HEADS
    return pl.pallas_call(
        _attn_kernel,
        grid=(b, MLA_HEADS // nh),
        in_specs=[pl.BlockSpec((1, s, nh * LANES), lambda bi, hg: (bi, 0, hg)),
                  pl.BlockSpec((1, s, nh * LANES // 2), lambda bi, hg: (bi, 0, hg)),
                  pl.BlockSpec((1, nh // 2, s, 4 * LANES), lambda bi, hg: (bi, hg, 0, 0)),
                  pl.BlockSpec((1, s // tk, nh * VT_ROWS, tk), lambda bi, hg: (bi, 0, hg, 0))],
        out_specs=pl.BlockSpec((1, s, nh * MLA_V), lambda bi, hg: (bi, 0, hg)),
        out_shape=jax.ShapeDtypeStruct((b, s, MLA_HEADS * MLA_V), BF16),
        scratch_shapes=[pltpu.VMEM((nh, 2 * LANES, tq), BF16), pltpu.VMEM((nh, tk, tq), F32),
                        pltpu.VMEM((nh, tk, tq), F32), pltpu.VMEM((nh, tk, tq), BF16),
                        pltpu.VMEM((nh, tk, tq), BF16), pltpu.VMEM((nh, 1, tq), F32), pltpu.VMEM((nh, 1, tq), F32),
                        pltpu.VMEM((nh, 1, tq), F32), pltpu.VMEM((nh, 1, tq), F32),
                        pltpu.VMEM((nh, MLA_V, tq), F32)],
        compiler_params=_params("parallel", "parallel"),
        name="mla_attention",
    )(qn, qr, kf, vt)


def _hgrn_proj_kernel(h_ref, g_ref, win_ref, lb_ref, q_ref, f_ref, v_ref, gate_ref):
    hk = HGRN_HEADS * HGRN_DK
    a = _rms(h_ref[...], g_ref[...]).astype(BF16)
    lb = lb_ref[...]
    qx = _dot(a, win_ref[:, :hk])
    q_ref[...] = qx * jax.nn.sigmoid(qx)
    fx = _dot(a, win_ref[:, hk:2 * hk])
    f_ref[...] = lb + (1.0 - lb) * jax.nn.sigmoid(fx)
    v_ref[...] = _dot(a, win_ref[:, 2 * hk:3 * hk]).astype(BF16)
    gx = _dot(a, win_ref[:, 3 * hk:])
    gate_ref[...] = (gx * jax.nn.sigmoid(gx)).astype(BF16)


def _hgrn_proj(h, gain, w_in, slot, lb):
    t = h.shape[0]
    tm = ROW_TILE
    hk = HGRN_HEADS * HGRN_DK
    row = pl.BlockSpec((tm, hk), lambda i: (i, 0))
    return pl.pallas_call(
        _hgrn_proj_kernel,
        grid=(t // tm,),
        in_specs=[row, _const_spec(gain.shape), _layer_spec(w_in, slot), _const_spec(lb.shape)],
        out_specs=[row, row, row, row],
        out_shape=[jax.ShapeDtypeStruct((t, hk), F32), jax.ShapeDtypeStruct((t, hk), F32),
                   jax.ShapeDtypeStruct((t, hk), BF16), jax.ShapeDtypeStruct((t, hk), BF16)],
        compiler_params=_params("parallel"),
        name="hgrn_proj",
    )(h, gain, w_in, lb)


def _chunk_cumsum(x, pos):
    shift = 1
    while shift < HGRN_CHUNK:
        x = x + jnp.where(pos >= shift, pltpu.roll(x, shift, 0), 0.0)
        shift *= 2
    return x


def _hgrn_rec_kernel(q_ref, f_ref, v_ref, gate_ref, on_ref, y_ref, state_ref):
    c = HGRN_CHUNK
    tl = HGRN_TILE
    nsub = tl // c

    @pl.when(pl.program_id(2) == 0)
    def _():
        state_ref[...] = jnp.zeros_like(state_ref)

    heads = range(HGRN_STEP_HEADS)
    chunks = [slice(n * c, (n + 1) * c) for n in range(nsub)]
    pos = lax.broadcasted_iota(jnp.int32, (tl, HGRN_DK), 0) % c
    row = lax.broadcasted_iota(jnp.int32, (tl, tl), 0)
    col = lax.broadcasted_iota(jnp.int32, (tl, tl), 1)
    keep = (row // c == col // c) & (row >= col)

    vs, q_decs, decays, scores, kvts = [], [], [], [], []
    for hd in heads:
        cols = slice(hd * HGRN_DK, (hd + 1) * HGRN_DK)
        q = q_ref[0, :, cols]
        f = f_ref[0, :, cols]
        v = v_ref[0, :, cols]
        k = 1.0 - f
        b = _chunk_cumsum(jnp.log2(f), pos)

        def per_chunk(fn, b=b):
            return jnp.concatenate([jnp.broadcast_to(fn(b[rows, :]), (c, HGRN_DK)) for rows in chunks], axis=0)

        b_ref = per_chunk(lambda bc: bc[c // 2:c // 2 + 1])
        q_rel = q * jnp.exp2(b - b_ref)
        k_rel = k * jnp.exp2(b_ref - b)
        q_dec = q_rel * per_chunk(lambda bc: jnp.exp2(bc[c // 2:c // 2 + 1]))
        k_dec = (k_rel * per_chunk(lambda bc: jnp.exp2(bc[c - 1:c] - bc[c // 2:c // 2 + 1]))).astype(BF16)
        scores.append(_dot_nt(q_rel.astype(BF16), k_rel.astype(BF16)))
        kvts.append([_dot_tn(v[rows], k_dec[rows]) for rows in chunks])
        q_decs.append(q_dec.astype(BF16))
        decays.append([jnp.exp2(b[rows.stop - 1:rows.stop, :]) for rows in chunks])
        vs.append(v)

    for hd in heads:
        st = state_ref[hd]
        states = []
        for n in range(nsub):
            states.append(st.astype(BF16))
            st = st * decays[hd][n] + kvts[hd][n]
        state_ref[hd] = st
        o = _dot(jnp.where(keep, scores[hd], 0.0).astype(BF16), vs[hd])
        o = o + jnp.concatenate([_dot_nt(q_decs[hd][rows], states[n]) for n, rows in enumerate(chunks)], axis=0)
        cols = slice(hd * HGRN_DV, (hd + 1) * HGRN_DV)
        y = _rms(o, on_ref[...]) * gate_ref[0, :, cols].astype(F32)
        y_ref[0, :, cols] = y.astype(y_ref.dtype)


def _hgrn_rec(q, f, v, gate, o_gain):
    b, s, _ = q.shape
    nh = HGRN_STEP_HEADS
    blk = pl.BlockSpec((1, HGRN_TILE, nh * HGRN_DK), lambda bi, hg, i: (bi, i, hg))
    return pl.pallas_call(
        _hgrn_rec_kernel,
        grid=(b, HGRN_HEADS // nh, s // HGRN_TILE),
        in_specs=[blk, blk, blk, blk, _const_spec(o_gain.shape)],
        out_specs=blk,
        out_shape=jax.ShapeDtypeStruct((b, s, HGRN_HEADS * HGRN_DV), BF16),
        scratch_shapes=[pltpu.VMEM((nh, HGRN_DV, HGRN_DK), F32)],
        compiler_params=_params("parallel", "parallel", "arbitrary"),
        name="hgrn_recurrence",
    )(q, f, v, gate, o_gain)


def _post_mlp_kernel(h_ref, y_ref, wo_ref, g_ref, w1_ref, w2_ref, out_ref, u_ref):
    g = g_ref[...]
    h1 = h_ref[...] + _rms(_dot(y_ref[...], wo_ref[...]), g[1:2])
    a = _rms(h1, g[2:3]).astype(BF16)
    for c in range(D_FF // FF_CHUNK):
        cols = slice(c * FF_CHUNK, (c + 1) * FF_CHUNK)
        u = jnp.maximum(_dot(a, w1_ref[:, cols]), 0.0)
        u_ref[:, cols] = (u * u).astype(BF16)
    out_ref[...] = h1 + _rms(_dot(u_ref[...], w2_ref[...]), g[3:4])


def _post_mlp(h, y, w_o, slot, gains, w1, w2, layer):
    t = h.shape[0]
    tm = ROW_TILE
    row = pl.BlockSpec((tm, D_MODEL), lambda i: (i, 0))
    return pl.pallas_call(
        _post_mlp_kernel,
        grid=(t // tm,),
        in_specs=[row, row, _layer_spec(w_o, slot), _layer_spec(gains, layer), _layer_spec(w1, layer),
                  _layer_spec(w2, layer)],
        out_specs=row,
        out_shape=jax.ShapeDtypeStruct((t, D_MODEL), F32),
        scratch_shapes=[pltpu.VMEM((tm, D_FF), BF16)],
        compiler_params=_params("parallel"),
        name="post_mlp",
    )(h, y, w_o, gains, w1, w2)


def _swap_halves(w):
    half = MLA_ROPE // 2
    g = w.reshape(w.shape[0], -1, 2, half)
    return g[:, :, ::-1, :].reshape(w.shape)


def _mla_weights(w_in, w_uq, w_ukv):
    nq, nkv = MLA_Q_LORA, MLA_KV_LORA
    kr = w_in[:, nq + nkv:]
    krs = _swap_halves(kr)
    w_in_x = jnp.concatenate([w_in[:, :nq + nkv], kr, kr, krs, krs], axis=1).astype(BF16)
    uq = w_uq.reshape(nq, MLA_HEADS, MLA_NOPE + MLA_ROPE)
    uq_nope = uq[:, :, :MLA_NOPE].reshape(nq, -1)
    uq_rope = uq[:, :, MLA_NOPE:].reshape(nq, -1)
    w_uq_x = jnp.concatenate([uq_nope, uq_rope, _swap_halves(uq_rope)], axis=1).astype(BF16)
    ukv = w_ukv.reshape(nkv, MLA_HEADS, MLA_NOPE + MLA_V)
    w_ukv_x = jnp.concatenate([ukv[:, :, :MLA_NOPE].reshape(nkv, -1), ukv[:, :, MLA_NOPE:].reshape(nkv, -1)],
                              axis=1).astype(BF16)
    return w_in_x, w_uq_x, w_ukv_x


def kernel(x, positions, norm_gains, mla_w_in, mla_q_norm, mla_kv_norm, mla_w_uq, mla_w_ukv, mla_w_o, hgrn_w_in,
           hgrn_lb_logits, hgrn_o_norm, hgrn_w_o, mlp_w1, mlp_w2):
    bsz, seq, d = x.shape
    t = bsz * seq

    inv_freq = jnp.power(ROPE_BASE, -jnp.arange(0, MLA_ROPE, 2, dtype=F32) / MLA_ROPE)
    ang = positions.astype(F32).reshape(t, 1) * inv_freq
    cos, sin = lax.optimization_barrier((jnp.cos(ang), jnp.sin(ang)))
    cos_t = jnp.concatenate([cos, cos, cos, cos], axis=1)
    sin_t = jnp.concatenate([-sin, sin, -sin, sin], axis=1)
    p = jax.nn.softmax(hgrn_lb_logits.astype(F32), axis=0)
    lower_bounds = jnp.cumsum(p, axis=0) - p[0]
    q_scale = (MLA_NOPE + MLA_ROPE) ** -0.5 * math.log2(math.e)

    mla_w_o, hgrn_w_in, hgrn_w_o, mlp_w1, mlp_w2 = (w.astype(BF16) for w in (mla_w_o, hgrn_w_in, hgrn_w_o, mlp_w1, mlp_w2))

    h = x.reshape(t, d)
    for layer in range(DEPTH):
        slot = layer // N_MIXERS
        if layer % N_MIXERS == 0:
            w_in, w_uq, w_ukv = _mla_weights(mla_w_in[slot], mla_w_uq[slot], mla_w_ukv[slot])
            qn, qr, kf, vt = _mla_proj(h, norm_gains[layer, 0:1], w_in, mla_q_norm[slot][None],
                                       mla_kv_norm[slot][None], w_uq, w_ukv, cos_t, sin_t, q_scale, bsz)
            y = _attention(qn.reshape(bsz, seq, -1), qr.reshape(bsz, seq, -1), kf,
                           vt.reshape(bsz, seq // ATTN_TK, -1, ATTN_TK))
            w_o = mla_w_o
        else:
            q, f, v, gate = _hgrn_proj(h, norm_gains[layer, 0:1], hgrn_w_in, slot, lower_bounds[layer][None])
            shp = (bsz, seq, -1)
            y = _hgrn_rec(q.reshape(shp), f.reshape(shp), v.reshape(shp), gate.reshape(shp),
                          hgrn_o_norm[slot][None])
            w_o = hgrn_w_o
        h = _post_mlp(h, y.reshape(t, -1), w_o, slot, norm_gains, mlp_w1, mlp_w2, layer)
    return h.reshape(bsz, seq, d)
```

```python
import functools
import math

import jax
import jax.numpy as jnp
from jax import lax
from jax.experimental import pallas as pl
from jax.experimental.pallas import tpu as pltpu

D_MODEL = 1024
DEPTH = 4
N_MIXERS = 2

MLA_HEADS = 8
MLA_Q_LORA = 512
MLA_KV_LORA = 256
MLA_NOPE = 128
MLA_ROPE = 64
MLA_V = 128
ROPE_BASE = 10000.0

HGRN_HEADS = 8
HGRN_DK = D_MODEL // HGRN_HEADS
HGRN_DV = D_MODEL // HGRN_HEADS
HGRN_CHUNK = 32

D_FF = 4 * D_MODEL
EPS = 1e-6

LANES = 128
VMEM_LIMIT = 56 * 1024 * 1024

ROW_TILE = 512
ATTN_TQ = 512
ATTN_TK = 256
ATTN_HEADS = 2
ATTN_UNROLL = 4
VT_ROWS = MLA_V + 16
HGRN_TILE = 256
HGRN_STEP_HEADS = 8
FF_CHUNK = 1024
MLP_ROW_GROUPS = 2

F32 = jnp.float32
BF16 = jnp.bfloat16


def _rms(x, g):
    return x * lax.rsqrt(jnp.mean(x * x, axis=-1, keepdims=True) + EPS) * g


def _dot(a, b):
    return jnp.dot(a, b, preferred_element_type=F32)


def _dot_nt(a, b):
    return lax.dot_general(a, b, (((1,), (1,)), ((), ())), preferred_element_type=F32)


def _dot_tn(a, b):
    return lax.dot_general(a, b, (((0,), (0,)), ((), ())), preferred_element_type=F32)


def _const_spec(shape):
    return pl.BlockSpec(shape, lambda *_: (0,) * len(shape))


def _layer_spec(stack, layer):
    return pl.BlockSpec((None,) + stack.shape[1:], lambda *_: (layer,) + (0,) * (stack.ndim - 1),
                        pipeline_mode=pl.Buffered(1))


def _params(*sem):
    return pltpu.CompilerParams(dimension_semantics=sem, vmem_limit_bytes=VMEM_LIMIT)


def _mla_proj_kernel(h_ref, g_ref, win_ref, qg_ref, kvg_ref, wuq_ref, wukv_ref, cos_ref, sin_ref,
                     qn_ref, qr_ref, kf_ref, vt_ref, *, q_scale):
    nq, nkv = MLA_Q_LORA, MLA_KV_LORA
    hn = MLA_HEADS * MLA_NOPE
    n_pair = MLA_HEADS * MLA_ROPE // LANES
    groups = [slice(c * ATTN_TK, (c + 1) * ATTN_TK) for c in range(vt_ref.shape[0])]
    a = [_rms(h_ref[r, :], g_ref[...]).astype(BF16) for r in groups]
    proj = [_dot(ak, win_ref[...]) for ak in a]
    c_q = [_rms(p[:, :nq], qg_ref[...]).astype(BF16) for p in proj]
    c_kv = [_rms(p[:, nq:nq + nkv], kvg_ref[...]).astype(BF16) for p in proj]
    q = [_dot(x, wuq_ref[...]) for x in c_q]
    kv = [_dot(x, wukv_ref[...]) for x in c_kv]
    ones = jnp.ones((VT_ROWS - MLA_V, ATTN_TK), BF16)
    lane = lax.broadcasted_iota(jnp.int32, (ATTN_TK, LANES), 1)
    for c, r in enumerate(groups):
        cos = cos_ref[r, :]
        sin = sin_ref[r, :]
        qn_ref[r, :] = (q[c][:, :hn] * q_scale).astype(BF16)
        for p in range(n_pair):
            qp = q[c][:, hn + p * LANES:hn + (p + 1) * LANES]
            qs = q[c][:, hn + (n_pair + p) * LANES:hn + (n_pair + p + 1) * LANES]
            qr_ref[r, p * LANES:(p + 1) * LANES] = ((qp * cos + qs * sin) * q_scale).astype(BF16)
        k_rope = proj[c][:, nq + nkv:nq + nkv + LANES] * cos + proj[c][:, nq + nkv + LANES:] * sin
        kr_even = jnp.where(lane < MLA_ROPE, k_rope, 0.0).astype(BF16)
        kr_odd = jnp.where(lane >= MLA_ROPE, k_rope, 0.0).astype(BF16)
        vt = kv[c][:, hn:].T.astype(BF16)
        for hd in range(MLA_HEADS):
            vt_ref[c, hd * VT_ROWS:hd * VT_ROWS + MLA_V, :] = vt[hd * MLA_V:(hd + 1) * MLA_V, :]
            vt_ref[c, hd * VT_ROWS + MLA_V:(hd + 1) * VT_ROWS, :] = ones
            off = (hd % 2) * 2 * LANES
            kf_ref[0, hd // 2, r, off:off + LANES] = kv[c][:, hd * LANES:(hd + 1) * LANES].astype(BF16)
            kf_ref[0, hd // 2, r, off + LANES:off + 2 * LANES] = kr_even if hd % 2 == 0 else kr_odd


def _mla_proj(h, gain, w_in, q_gain, kv_gain, w_uq, w_ukv, cos, sin, q_scale, bsz):
    t = h.shape[0]
    tm = ROW_TILE
    n_s = t // bsz // tm
    row = lambda w: pl.BlockSpec((tm, w), lambda i: (i, 0))
    hn = MLA_HEADS * MLA_NOPE
    return pl.pallas_call(
        functools.partial(_mla_proj_kernel, q_scale=q_scale),
        grid=(t // tm,),
        in_specs=[row(D_MODEL), _const_spec(gain.shape), _const_spec(w_in.shape), _const_spec(q_gain.shape),
                  _const_spec(kv_gain.shape), _const_spec(w_uq.shape), _const_spec(w_ukv.shape),
                  row(LANES), row(LANES)],
        out_specs=[row(hn), row(MLA_HEADS * MLA_ROPE),
                   pl.BlockSpec((1, MLA_HEADS // 2, tm, 4 * LANES), lambda i: (i // n_s, 0, i % n_s, 0)),
                   pl.BlockSpec((tm // ATTN_TK, MLA_HEADS * VT_ROWS, ATTN_TK), lambda i: (i, 0, 0))],
        out_shape=[jax.ShapeDtypeStruct((t, hn), BF16), jax.ShapeDtypeStruct((t, MLA_HEADS * MLA_ROPE), BF16),
                   jax.ShapeDtypeStruct((bsz, MLA_HEADS // 2, t // bsz, 4 * LANES), BF16),
                   jax.ShapeDtypeStruct((t // ATTN_TK, MLA_HEADS * VT_ROWS, ATTN_TK), BF16)],
        compiler_params=_params("parallel"),
        name="mla_proj",
    )(h, gain, w_in, q_gain, kv_gain, w_uq, w_ukv, cos, sin)


def _attn_kernel(qn_ref, qr_ref, k_ref, vt_ref, o_ref, qt_ref, sa_ref, sb_ref, pa_ref, pb_ref, aa_ref, ab_ref,
                 m_ref, l_ref, acc_ref):
    tq, tk = ATTN_TQ, ATTN_TK
    diag = tq // tk
    heads = range(ATTN_HEADS)

    def scores(j, st_ref):
        start = pl.multiple_of(j * tk, tk)
        for hd in heads:
            kc = (hd % 2) * 2 * LANES
            st_ref[hd] = _dot(k_ref[0, hd // 2, pl.ds(start, tk), kc:kc + 2 * LANES], qt_ref[hd])

    def softmax(st_ref, pt_ref, alpha_ref, mask_from=None):
        for hd in heads:
            st = st_ref[hd]
            if mask_from is not None:
                key = lax.broadcasted_iota(jnp.int32, st.shape, 0) + mask_from
                qry = lax.broadcasted_iota(jnp.int32, st.shape, 1)
                st = jnp.where(qry >= key, st, -jnp.inf)
            m = m_ref[hd]
            m_new = jnp.maximum(m, jnp.max(st, axis=0, keepdims=True))
            alpha = jnp.exp2(m - m_new)
            pt = jnp.exp2(st - m_new)
            m_ref[hd] = m_new
            pt_ref[hd] = pt.astype(BF16)
            alpha_ref[hd] = alpha

    def accumulate(j, pt_ref, alpha_ref):
        for hd in heads:
            pv = _dot(vt_ref[0, j, hd * VT_ROWS:(hd + 1) * VT_ROWS, :], pt_ref[hd])
            acc_ref[hd] = alpha_ref[hd] * acc_ref[hd] + pv[:MLA_V]
            l_ref[hd] = alpha_ref[hd] * l_ref[hd] + pv[MLA_V:MLA_V + 1]

    bufs = ((sa_ref, pa_ref, aa_ref), (sb_ref, pb_ref, ab_ref))

    def run(j0, n, last):
        for idx in range(n):
            (s_cur, p_cur, a_cur), (s_nxt, p_prv, a_prv) = bufs[idx % 2], bufs[1 - idx % 2]
            if not (last and idx == n - 1):
                scores(j0 + idx + 1, s_nxt)
            accumulate(jnp.maximum(j0 + idx - 1, 0), p_prv, a_prv)
            on_diag = last and idx >= n - diag
            softmax(s_cur, p_cur, a_cur, mask_from=(idx - (n - diag)) * tk if on_diag else None)
        if last:
            accumulate(j0 + n - 1, p_cur, a_cur)

    def group(g, carry):
        run(ATTN_UNROLL * g, ATTN_UNROLL, last=False)
        return carry

    def tile(i, carry):
        rows = pl.ds(pl.multiple_of(i * tq, tq), tq)
        for hd in heads:
            qt_ref[hd, :LANES, :] = qn_ref[0, rows, hd * LANES:(hd + 1) * LANES].T
            qt_ref[hd, LANES:, :] = qr_ref[0, rows, (hd // 2) * LANES:(hd // 2 + 1) * LANES].T
        m_ref[...] = jnp.full(m_ref.shape, -jnp.inf, F32)
        l_ref[...] = jnp.zeros(l_ref.shape, F32)
        acc_ref[...] = jnp.zeros(acc_ref.shape, F32)
        pb_ref[...] = jnp.zeros(pb_ref.shape, BF16)
        ab_ref[...] = jnp.ones(ab_ref.shape, F32)

        below = i * diag
        scores(0, sa_ref)
        lax.fori_loop(0, below // ATTN_UNROLL, group, 0)
        for rest in range(0, ATTN_UNROLL, diag):
            @pl.when(below % ATTN_UNROLL == rest)
            def _(rest=rest):
                run(below - rest, rest + diag, last=True)

        for hd in heads:
            o_ref[0, rows, hd * MLA_V:(hd + 1) * MLA_V] = (acc_ref[hd] / l_ref[hd]).T.astype(o_ref.dtype)
        return carry

    lax.fori_loop(0, qn_ref.shape[1] // tq, tile, 0)


def _attention(qn, qr, kf, vt):
    b, s, _ = qn.shape
    tq, tk = ATTN_TQ, ATTN_TK
    nh = ATTN_HEADS
    return pl.pallas_call(
        _attn_kernel,
        grid=(b, MLA_HEADS // nh),
        in_specs=[pl.BlockSpec((1, s, nh * LANES), lambda bi, hg: (bi, 0, hg)),
                  pl.BlockSpec((1, s, nh * LANES // 2), lambda bi, hg: (bi, 0, hg)),
                  pl.BlockSpec((1, nh // 2, s, 4 * LANES), lambda bi, hg: (bi, hg, 0, 0)),
                  pl.BlockSpec((1, s // tk, nh * VT_ROWS, tk), lambda bi, hg: (bi, 0, hg, 0))],
        out_specs=pl.BlockSpec((1, s, nh * MLA_V), lambda bi, hg: (bi, 0, hg)),
        out_shape=jax.ShapeDtypeStruct((b, s, MLA_HEADS * MLA_V), BF16),
        scratch_shapes=[pltpu.VMEM((nh, 2 * LANES, tq), BF16), pltpu.VMEM((nh, tk, tq), F32),
                        pltpu.VMEM((nh, tk, tq), F32), pltpu.VMEM((nh, tk, tq), BF16),
                        pltpu.VMEM((nh, tk, tq), BF16), pltpu.VMEM((nh, 1, tq), F32), pltpu.VMEM((nh, 1, tq), F32),
                        pltpu.VMEM((nh, 1, tq), F32), pltpu.VMEM((nh, 1, tq), F32),
                        pltpu.VMEM((nh, MLA_V, tq), F32)],
        compiler_params=_params("parallel", "parallel"),
        name="mla_attention",
    )(qn, qr, kf, vt)


def _hgrn_proj_kernel(h_ref, g_ref, win_ref, lb_ref, q_ref, f_ref, v_ref, gate_ref):
    hk = HGRN_HEADS * HGRN_DK
    tm = h_ref.shape[0]
    lb = lb_ref[...]
    groups = [slice(k * tm // MLP_ROW_GROUPS, (k + 1) * tm // MLP_ROW_GROUPS) for k in range(MLP_ROW_GROUPS)]
    a = [_rms(h_ref[r, :], g_ref[...]).astype(BF16) for r in groups]
    for r, ak in zip(groups, a):
        qx = _dot(ak, win_ref[:, :hk])
        q_ref[r, :] = qx * jax.nn.sigmoid(qx)
    for r, ak in zip(groups, a):
        fx = _dot(ak, win_ref[:, hk:2 * hk])
        f_ref[r, :] = lb + (1.0 - lb) * jax.nn.sigmoid(fx)
    for r, ak in zip(groups, a):
        v_ref[r, :] = _dot(ak, win_ref[:, 2 * hk:3 * hk]).astype(BF16)
    for r, ak in zip(groups, a):
        gx = _dot(ak, win_ref[:, 3 * hk:])
        gate_ref[r, :] = (gx * jax.nn.sigmoid(gx)).astype(BF16)


def _hgrn_proj(h, gain, w_in, slot, lb):
    t = h.shape[0]
    tm = ROW_TILE
    hk = HGRN_HEADS * HGRN_DK
    row = pl.BlockSpec((tm, hk), lambda i: (i, 0))
    return pl.pallas_call(
        _hgrn_proj_kernel,
        grid=(t // tm,),
        in_specs=[row, _const_spec(gain.shape), _layer_spec(w_in, slot), _const_spec(lb.shape)],
        out_specs=[row, row, row, row],
        out_shape=[jax.ShapeDtypeStruct((t, hk), F32), jax.ShapeDtypeStruct((t, hk), F32),
                   jax.ShapeDtypeStruct((t, hk), BF16), jax.ShapeDtypeStruct((t, hk), BF16)],
        compiler_params=_params("parallel"),
        name="hgrn_proj",
    )(h, gain, w_in, lb)


def _chunk_cumsum(x, pos):
    shift = 1
    while shift < HGRN_CHUNK:
        x = x + jnp.where(pos >= shift, pltpu.roll(x, shift, 0), 0.0)
        shift *= 2
    return x


def _hgrn_rec_kernel(q_ref, f_ref, v_ref, gate_ref, on_ref, y_ref, state_ref):
    c = HGRN_CHUNK
    tl = HGRN_TILE
    nsub = tl // c

    @pl.when(pl.program_id(2) == 0)
    def _():
        state_ref[...] = jnp.zeros_like(state_ref)

    heads = range(HGRN_STEP_HEADS)
    chunks = [slice(n * c, (n + 1) * c) for n in range(nsub)]
    pos = lax.broadcasted_iota(jnp.int32, (tl, HGRN_DK), 0) % c
    row = lax.broadcasted_iota(jnp.int32, (tl, tl), 0)
    col = lax.broadcasted_iota(jnp.int32, (tl, tl), 1)
    keep = (row // c == col // c) & (row >= col)

    vs, q_decs, decays, scores, kvts = [], [], [], [], []
    for hd in heads:
        cols = slice(hd * HGRN_DK, (hd + 1) * HGRN_DK)
        q = q_ref[0, :, cols]
        f = f_ref[0, :, cols]
        v = v_ref[0, :, cols]
        k = 1.0 - f
        b = _chunk_cumsum(jnp.log2(f), pos)

        def per_chunk(fn, b=b):
            return jnp.concatenate([jnp.broadcast_to(fn(b[rows, :]), (c, HGRN_DK)) for rows in chunks], axis=0)

        b_ref = per_chunk(lambda bc: bc[c // 2:c // 2 + 1])
        q_rel = q * jnp.exp2(b - b_ref)
        k_rel = k * jnp.exp2(b_ref - b)
        q_dec = q_rel * per_chunk(lambda bc: jnp.exp2(bc[c // 2:c // 2 + 1]))
        k_dec = (k_rel * per_chunk(lambda bc: jnp.exp2(bc[c - 1:c] - bc[c // 2:c // 2 + 1]))).astype(BF16)
        scores.append(_dot_nt(q_rel.astype(BF16), k_rel.astype(BF16)))
        kvts.append([_dot_tn(v[rows], k_dec[rows]) for rows in chunks])
        q_decs.append(q_dec.astype(BF16))
        decays.append([jnp.exp2(b[rows.stop - 1:rows.stop, :]) for rows in chunks])
        vs.append(v)

    for hd in heads:
        st = state_ref[hd]
        states = []
        for n in range(nsub):
            states.append(st.astype(BF16))
            st = st * decays[hd][n] + kvts[hd][n]
        state_ref[hd] = st
        o = _dot(jnp.where(keep, scores[hd], 0.0).astype(BF16), vs[hd])
        o = o + jnp.concatenate([_dot_nt(q_decs[hd][rows], states[n]) for n, rows in enumerate(chunks)], axis=0)
        cols = slice(hd * HGRN_DV, (hd + 1) * HGRN_DV)
        y = _rms(o, on_ref[...]) * gate_ref[0, :, cols].astype(F32)
        y_ref[0, :, cols] = y.astype(y_ref.dtype)


def _hgrn_rec(q, f, v, gate, o_gain):
    b, s, _ = q.shape
    nh = HGRN_STEP_HEADS
    blk = pl.BlockSpec((1, HGRN_TILE, nh * HGRN_DK), lambda bi, hg, i: (bi, i, hg))
    return pl.pallas_call(
        _hgrn_rec_kernel,
        grid=(b, HGRN_HEADS // nh, s // HGRN_TILE),
        in_specs=[blk, blk, blk, blk, _const_spec(o_gain.shape)],
        out_specs=blk,
        out_shape=jax.ShapeDtypeStruct((b, s, HGRN_HEADS * HGRN_DV), BF16),
        scratch_shapes=[pltpu.VMEM((nh, HGRN_DV, HGRN_DK), F32)],
        compiler_params=_params("parallel", "parallel", "arbitrary"),
        name="hgrn_recurrence",
    )(q, f, v, gate, o_gain)


def _post_mlp_kernel(h_ref, y_ref, wo_ref, g_ref, w1_ref, w2_ref, out_ref, u_ref):
    g = g_ref[...]
    tm = h_ref.shape[0]
    groups = [slice(k * tm // MLP_ROW_GROUPS, (k + 1) * tm // MLP_ROW_GROUPS) for k in range(MLP_ROW_GROUPS)]
    mixed = [_dot(y_ref[r, :], wo_ref[...]) for r in groups]
    h1 = [h_ref[r, :] + _rms(m, g[1:2]) for r, m in zip(groups, mixed)]
    a = [_rms(x, g[2:3]).astype(BF16) for x in h1]
    for c in range(D_FF // FF_CHUNK):
        cols = slice(c * FF_CHUNK, (c + 1) * FF_CHUNK)
        for r, ak in zip(groups, a):
            u = jnp.maximum(_dot(ak, w1_ref[:, cols]), 0.0)
            u_ref[r, cols] = (u * u).astype(BF16)
    down = [_dot(u_ref[r, :], w2_ref[...]) for r in groups]
    for r, x, d in zip(groups, h1, down):
        out_ref[r, :] = x + _rms(d, g[3:4])


def _post_mlp(h, y, w_o, slot, gains, w1, w2, layer):
    t = h.shape[0]
    tm = ROW_TILE
    row = pl.BlockSpec((tm, D_MODEL), lambda i: (i, 0))
    return pl.pallas_call(
        _post_mlp_kernel,
        grid=(t // tm,),
        in_specs=[row, row, _layer_spec(w_o, slot), _layer_spec(gains, layer), _layer_spec(w1, layer),
                  _layer_spec(w2, layer)],
        out_specs=row,
        out_shape=jax.ShapeDtypeStruct((t, D_MODEL), F32),
        scratch_shapes=[pltpu.VMEM((tm, D_FF), BF16)],
        compiler_params=_params("parallel"),
        name="post_mlp",
    )(h, y, w_o, gains, w1, w2)


def _swap_halves(w):
    half = MLA_ROPE // 2
    g = w.reshape(w.shape[0], -1, 2, half)
    return g[:, :, ::-1, :].reshape(w.shape)


def _mla_weights(w_in, w_uq, w_ukv):
    nq, nkv = MLA_Q_LORA, MLA_KV_LORA
    kr = w_in[:, nq + nkv:]
    krs = _swap_halves(kr)
    w_in_x = jnp.concatenate([w_in[:, :nq + nkv], kr, kr, krs, krs], axis=1).astype(BF16)
    uq = w_uq.reshape(nq, MLA_HEADS, MLA_NOPE + MLA_ROPE)
    uq_nope = uq[:, :, :MLA_NOPE].reshape(nq, -1)
    uq_rope = uq[:, :, MLA_NOPE:].reshape(nq, -1)
    w_uq_x = jnp.concatenate([uq_nope, uq_rope, _swap_halves(uq_rope)], axis=1).astype(BF16)
    ukv = w_ukv.reshape(nkv, MLA_HEADS, MLA_NOPE + MLA_V)
    w_ukv_x = jnp.concatenate([ukv[:, :, :MLA_NOPE].reshape(nkv, -1), ukv[:, :, MLA_NOPE:].reshape(nkv, -1)],
                              axis=1).astype(BF16)
    return w_in_x, w_uq_x, w_ukv_x


def kernel(x, positions, norm_gains, mla_w_in, mla_q_norm, mla_kv_norm, mla_w_uq, mla_w_ukv, mla_w_o, hgrn_w_in,
           hgrn_lb_logits, hgrn_o_norm, hgrn_w_o, mlp_w1, mlp_w2):
    bsz, seq, d = x.shape
    t = bsz * seq

    inv_freq = jnp.power(ROPE_BASE, -jnp.arange(0, MLA_ROPE, 2, dtype=F32) / MLA_ROPE)
    ang = positions.astype(F32).reshape(t, 1) * inv_freq
    ang = ang.reshape(-1, LANES)
    cos, sin = lax.optimization_barrier((jnp.cos(ang), jnp.sin(ang)))
    cos, sin = cos.reshape(t, -1), sin.reshape(t, -1)
    cos_t = jnp.concatenate([cos, cos, cos, cos], axis=1)
    sin_t = jnp.concatenate([-sin, sin, -sin, sin], axis=1)
    p = jax.nn.softmax(hgrn_lb_logits.astype(F32), axis=0)
    lower_bounds = jnp.cumsum(p, axis=0) - p[0]
    q_scale = (MLA_NOPE + MLA_ROPE) ** -0.5 * math.log2(math.e)

    mla_w_o, hgrn_w_in, hgrn_w_o, mlp_w1, mlp_w2 = (w.astype(BF16) for w in (mla_w_o, hgrn_w_in, hgrn_w_o, mlp_w1, mlp_w2))

    h = x.reshape(t, d)
    for layer in range(DEPTH):
        slot = layer // N_MIXERS
        if layer % N_MIXERS == 0:
            w_in, w_uq, w_ukv = _mla_weights(mla_w_in[slot], mla_w_uq[slot], mla_w_ukv[slot])
            qn, qr, kf, vt = _mla_proj(h, norm_gains[layer, 0:1], w_in, mla_q_norm[slot][None],
                                       mla_kv_norm[slot][None], w_uq, w_ukv, cos_t, sin_t, q_scale, bsz)
            y = _attention(qn.reshape(bsz, seq, -1), qr.reshape(bsz, seq, -1), kf,
                           vt.reshape(bsz, seq // ATTN_TK, -1, ATTN_TK))
            w_o = mla_w_o
        else:
            q, f, v, gate = _hgrn_proj(h, norm_gains[layer, 0:1], hgrn_w_in, slot, lower_bounds[layer][None])
            shp = (bsz, seq, -1)
            y = _hgrn_rec(q.reshape(shp), f.reshape(shp), v.reshape(shp), gate.reshape(shp),
                          hgrn_o_norm[slot][None])
            w_o = hgrn_w_o
        h = _post_mlp(h, y.reshape(t, -1), w_o, slot, norm_gains, mlp_w1, mlp_w2, layer)
    return h.reshape(bsz, seq, d)
```

```python
import functools
import math

import jax
import jax.numpy as jnp
from jax import lax
from jax.experimental import pallas as pl
from jax.experimental.pallas import tpu as pltpu

D_MODEL = 1024
DEPTH = 4
N_MIXERS = 2

MLA_HEADS = 8
MLA_Q_LORA = 512
MLA_KV_LORA = 256
MLA_NOPE = 128
MLA_ROPE = 64
MLA_V = 128
ROPE_BASE = 10000.0

HGRN_HEADS = 8
HGRN_DK = D_MODEL // HGRN_HEADS
HGRN_DV = D_MODEL // HGRN_HEADS
HGRN_CHUNK = 32

D_FF = 4 * D_MODEL
EPS = 1e-6

LANES = 128
VMEM_LIMIT = 56 * 1024 * 1024

ROW_TILE = 512
ATTN_TQ = 512
ATTN_TK = 256
ATTN_HEADS = 2
ATTN_UNROLL = 8
VT_ROWS = MLA_V + 16
HGRN_TILE = 256
HGRN_STEP_HEADS = 8
FF_CHUNK = 1024
MLP_ROW_GROUPS = 2

F32 = jnp.float32
BF16 = jnp.bfloat16


def _rms(x, g):
    return x * lax.rsqrt(jnp.mean(x * x, axis=-1, keepdims=True) + EPS) * g


def _dot(a, b):
    return jnp.dot(a, b, preferred_element_type=F32)


def _dot_nt(a, b):
    return lax.dot_general(a, b, (((1,), (1,)), ((), ())), preferred_element_type=F32)


def _dot_tn(a, b):
    return lax.dot_general(a, b, (((0,), (0,)), ((), ())), preferred_element_type=F32)


def _const_spec(shape):
    return pl.BlockSpec(shape, lambda *_: (0,) * len(shape))


def _layer_spec(stack, layer):
    return pl.BlockSpec((None,) + stack.shape[1:], lambda *_: (layer,) + (0,) * (stack.ndim - 1),
                        pipeline_mode=pl.Buffered(1))


def _params(*sem):
    return pltpu.CompilerParams(dimension_semantics=sem, vmem_limit_bytes=VMEM_LIMIT)


def _mla_proj_kernel(h_ref, g_ref, win_ref, qg_ref, kvg_ref, wuq_ref, wukv_ref, cos_ref, sin_ref,
                     qn_ref, qr_ref, kf_ref, vt_ref, *, q_scale):
    nq, nkv = MLA_Q_LORA, MLA_KV_LORA
    hn = MLA_HEADS * MLA_NOPE
    n_pair = MLA_HEADS * MLA_ROPE // LANES
    groups = [slice(c * ATTN_TK, (c + 1) * ATTN_TK) for c in range(vt_ref.shape[0])]
    a = [_rms(h_ref[r, :], g_ref[...]).astype(BF16) for r in groups]
    proj = [_dot(ak, win_ref[...]) for ak in a]
    c_q = [_rms(p[:, :nq], qg_ref[...]).astype(BF16) for p in proj]
    c_kv = [_rms(p[:, nq:nq + nkv], kvg_ref[...]).astype(BF16) for p in proj]
    q = [_dot(x, wuq_ref[...]) for x in c_q]
    kv = [_dot(x, wukv_ref[...]) for x in c_kv]
    ones = jnp.ones((VT_ROWS - MLA_V, ATTN_TK), BF16)
    lane = lax.broadcasted_iota(jnp.int32, (ATTN_TK, LANES), 1)
    for c, r in enumerate(groups):
        cos = cos_ref[r, :]
        sin = sin_ref[r, :]
        qn_ref[r, :] = (q[c][:, :hn] * q_scale).astype(BF16)
        for p in range(n_pair):
            qp = q[c][:, hn + p * LANES:hn + (p + 1) * LANES]
            qs = q[c][:, hn + (n_pair + p) * LANES:hn + (n_pair + p + 1) * LANES]
            qr_ref[r, p * LANES:(p + 1) * LANES] = ((qp * cos + qs * sin) * q_scale).astype(BF16)
        k_rope = proj[c][:, nq + nkv:nq + nkv + LANES] * cos + proj[c][:, nq + nkv + LANES:] * sin
        kr_even = jnp.where(lane < MLA_ROPE, k_rope, 0.0).astype(BF16)
        kr_odd = jnp.where(lane >= MLA_ROPE, k_rope, 0.0).astype(BF16)
        vt = kv[c][:, hn:].T.astype(BF16)
        for hd in range(MLA_HEADS):
            vt_ref[c, hd * VT_ROWS:hd * VT_ROWS + MLA_V, :] = vt[hd * MLA_V:(hd + 1) * MLA_V, :]
            vt_ref[c, hd * VT_ROWS + MLA_V:(hd + 1) * VT_ROWS, :] = ones
            off = (hd % 2) * 2 * LANES
            kf_ref[0, hd // 2, r, off:off + LANES] = kv[c][:, hd * LANES:(hd + 1) * LANES].astype(BF16)
            kf_ref[0, hd // 2, r, off + LANES:off + 2 * LANES] = kr_even if hd % 2 == 0 else kr_odd


def _mla_proj(h, gain, w_in, q_gain, kv_gain, w_uq, w_ukv, cos, sin, q_scale, bsz):
    t = h.shape[0]
    tm = ROW_TILE
    n_s = t // bsz // tm
    row = lambda w: pl.BlockSpec((tm, w), lambda i: (i, 0))
    hn = MLA_HEADS * MLA_NOPE
    return pl.pallas_call(
        functools.partial(_mla_proj_kernel, q_scale=q_scale),
        grid=(t // tm,),
        in_specs=[row(D_MODEL), _const_spec(gain.shape), _const_spec(w_in.shape), _const_spec(q_gain.shape),
                  _const_spec(kv_gain.shape), _const_spec(w_uq.shape), _const_spec(w_ukv.shape),
                  row(LANES), row(LANES)],
        out_specs=[row(hn), row(MLA_HEADS * MLA_ROPE),
                   pl.BlockSpec((1, MLA_HEADS // 2, tm, 4 * LANES), lambda i: (i // n_s, 0, i % n_s, 0)),
                   pl.BlockSpec((tm // ATTN_TK, MLA_HEADS * VT_ROWS, ATTN_TK), lambda i: (i, 0, 0))],
        out_shape=[jax.ShapeDtypeStruct((t, hn), BF16), jax.ShapeDtypeStruct((t, MLA_HEADS * MLA_ROPE), BF16),
                   jax.ShapeDtypeStruct((bsz, MLA_HEADS // 2, t // bsz, 4 * LANES), BF16),
                   jax.ShapeDtypeStruct((t // ATTN_TK, MLA_HEADS * VT_ROWS, ATTN_TK), BF16)],
        compiler_params=_params("parallel"),
        name="mla_proj",
    )(h, gain, w_in, q_gain, kv_gain, w_uq, w_ukv, cos, sin)


def _attn_kernel(qn_ref, qr_ref, k_ref, vt_ref, o_ref, qt_ref, sa_ref, sb_ref, pa_ref, pb_ref, aa_ref, ab_ref,
                 m_ref, l_ref, acc_ref):
    tq, tk = ATTN_TQ, ATTN_TK
    diag = tq // tk
    heads = range(ATTN_HEADS)

    def scores(j, st_ref):
        start = pl.multiple_of(j * tk, tk)
        for hd in heads:
            kc = (hd % 2) * 2 * LANES
            st_ref[hd] = _dot(k_ref[0, hd // 2, pl.ds(start, tk), kc:kc + 2 * LANES], qt_ref[hd])

    def softmax(st_ref, pt_ref, alpha_ref, mask_from=None):
        for hd in heads:
            st = st_ref[hd]
            if mask_from is not None:
                key = lax.broadcasted_iota(jnp.int32, st.shape, 0) + mask_from
                qry = lax.broadcasted_iota(jnp.int32, st.shape, 1)
                st = jnp.where(qry >= key, st, -jnp.inf)
            m = m_ref[hd]
            m_new = jnp.maximum(m, jnp.max(st, axis=0, keepdims=True))
            alpha = jnp.exp2(m - m_new)
            pt = jnp.exp2(st - m_new)
            m_ref[hd] = m_new
            pt_ref[hd] = pt.astype(BF16)
            alpha_ref[hd] = alpha

    def accumulate(j, pt_ref, alpha_ref):
        for hd in heads:
            pv = _dot(vt_ref[0, j, hd * VT_ROWS:(hd + 1) * VT_ROWS, :], pt_ref[hd])
            acc_ref[hd] = alpha_ref[hd] * acc_ref[hd] + pv[:MLA_V]
            l_ref[hd] = alpha_ref[hd] * l_ref[hd] + pv[MLA_V:MLA_V + 1]

    bufs = ((sa_ref, pa_ref, aa_ref), (sb_ref, pb_ref, ab_ref))

    def run(j0, n, last):
        for idx in range(n):
            (s_cur, p_cur, a_cur), (s_nxt, p_prv, a_prv) = bufs[idx % 2], bufs[1 - idx % 2]
            if not (last and idx == n - 1):
                scores(j0 + idx + 1, s_nxt)
            accumulate(jnp.maximum(j0 + idx - 1, 0), p_prv, a_prv)
            on_diag = last and idx >= n - diag
            softmax(s_cur, p_cur, a_cur, mask_from=(idx - (n - diag)) * tk if on_diag else None)
        if last:
            accumulate(j0 + n - 1, p_cur, a_cur)

    def group(g, carry):
        run(ATTN_UNROLL * g, ATTN_UNROLL, last=False)
        return carry

    n_tiles = qn_ref.shape[1] // tq

    def first_scores(i):
        rows = pl.ds(pl.multiple_of(i * tq, tq), tq)
        for hd in heads:
            qt_ref[hd, :LANES, :] = qn_ref[0, rows, hd * LANES:(hd + 1) * LANES].T
            qt_ref[hd, LANES:, :] = qr_ref[0, rows, (hd // 2) * LANES:(hd // 2 + 1) * LANES].T
        scores(0, sa_ref)

    def tile(i, carry):
        rows = pl.ds(pl.multiple_of(i * tq, tq), tq)
        m_ref[...] = jnp.full(m_ref.shape, -jnp.inf, F32)
        l_ref[...] = jnp.zeros(l_ref.shape, F32)
        acc_ref[...] = jnp.zeros(acc_ref.shape, F32)
        pb_ref[...] = jnp.zeros(pb_ref.shape, BF16)
        ab_ref[...] = jnp.ones(ab_ref.shape, F32)

        below = i * diag
        lax.fori_loop(0, below // ATTN_UNROLL, group, 0)
        for rest in range(0, ATTN_UNROLL, diag):
            @pl.when(below % ATTN_UNROLL == rest)
            def _(rest=rest):
                run(below - rest, rest + diag, last=True)

        first_scores(jnp.minimum(i + 1, n_tiles - 1))
        for hd in heads:
            o_ref[0, rows, hd * MLA_V:(hd + 1) * MLA_V] = (acc_ref[hd] / l_ref[hd]).T.astype(o_ref.dtype)
        return carry

    first_scores(0)
    lax.fori_loop(0, n_tiles, tile, 0)


def _attention(qn, qr, kf, vt):
    b, s, _ = qn.shape
    tq, tk = ATTN_TQ, ATTN_TK
    nh = ATTN_HEADS
    return pl.pallas_call(
        _attn_kernel,
        grid=(b, MLA_HEADS // nh),
        in_specs=[pl.BlockSpec((1, s, nh * LANES), lambda bi, hg: (bi, 0, hg)),
                  pl.BlockSpec((1, s, nh * LANES // 2), lambda bi, hg: (bi, 0, hg)),
                  pl.BlockSpec((1, nh // 2, s, 4 * LANES), lambda bi, hg: (bi, hg, 0, 0)),
                  pl.BlockSpec((1, s // tk, nh * VT_ROWS, tk), lambda bi, hg: (bi, 0, hg, 0))],
        out_specs=pl.BlockSpec((1, s, nh * MLA_V), lambda bi, hg: (bi, 0, hg)),
        out_shape=jax.ShapeDtypeStruct((b, s, MLA_HEADS * MLA_V), BF16),
        scratch_shapes=[pltpu.VMEM((nh, 2 * LANES, tq), BF16), pltpu.VMEM((nh, tk, tq), F32),
                        pltpu.VMEM((nh, tk, tq), F32), pltpu.VMEM((nh, tk, tq), BF16),
                        pltpu.VMEM((nh, tk, tq), BF16), pltpu.VMEM((nh, 1, tq), F32), pltpu.VMEM((nh, 1, tq), F32),
                        pltpu.VMEM((nh, 1, tq), F32), pltpu.VMEM((nh, 1, tq), F32),
                        pltpu.VMEM((nh, MLA_V, tq), F32)],
        compiler_params=_params("parallel", "parallel"),
        name="mla_attention",
    )(qn, qr, kf, vt)


def _hgrn_proj_kernel(h_ref, g_ref, win_ref, lb_ref, q_ref, f_ref, v_ref, gate_ref):
    hk = HGRN_HEADS * HGRN_DK
    tm = h_ref.shape[0]
    lb = lb_ref[...]
    groups = [slice(k * tm // MLP_ROW_GROUPS, (k + 1) * tm // MLP_ROW_GROUPS) for k in range(MLP_ROW_GROUPS)]
    a = [_rms(h_ref[r, :], g_ref[...]).astype(BF16) for r in groups]
    for r, ak in zip(groups, a):
        qx = _dot(ak, win_ref[:, :hk])
        q_ref[r, :] = qx * jax.nn.sigmoid(qx)
    for r, ak in zip(groups, a):
        fx = _dot(ak, win_ref[:, hk:2 * hk])
        f_ref[r, :] = lb + (1.0 - lb) * jax.nn.sigmoid(fx)
    for r, ak in zip(groups, a):
        v_ref[r, :] = _dot(ak, win_ref[:, 2 * hk:3 * hk]).astype(BF16)
    for r, ak in zip(groups, a):
        gx = _dot(ak, win_ref[:, 3 * hk:])
        gate_ref[r, :] = (gx * jax.nn.sigmoid(gx)).astype(BF16)


def _hgrn_proj(h, gain, w_in, slot, lb):
    t = h.shape[0]
    tm = ROW_TILE
    hk = HGRN_HEADS * HGRN_DK
    row = pl.BlockSpec((tm, hk), lambda i: (i, 0))
    return pl.pallas_call(
        _hgrn_proj_kernel,
        grid=(t // tm,),
        in_specs=[row, _const_spec(gain.shape), _layer_spec(w_in, slot), _const_spec(lb.shape)],
        out_specs=[row, row, row, row],
        out_shape=[jax.ShapeDtypeStruct((t, hk), F32), jax.ShapeDtypeStruct((t, hk), F32),
                   jax.ShapeDtypeStruct((t, hk), BF16), jax.ShapeDtypeStruct((t, hk), BF16)],
        compiler_params=_params("parallel"),
        name="hgrn_proj",
    )(h, gain, w_in, lb)


def _chunk_cumsum(x, pos):
    shift = 1
    while shift < HGRN_CHUNK:
        x = x + jnp.where(pos >= shift, pltpu.roll(x, shift, 0), 0.0)
        shift *= 2
    return x


def _hgrn_rec_kernel(q_ref, f_ref, v_ref, gate_ref, on_ref, y_ref, state_ref):
    c = HGRN_CHUNK
    tl = HGRN_TILE
    nsub = tl // c

    @pl.when(pl.program_id(2) == 0)
    def _():
        state_ref[...] = jnp.zeros_like(state_ref)

    heads = range(HGRN_STEP_HEADS)
    chunks = [slice(n * c, (n + 1) * c) for n in range(nsub)]
    pos = lax.broadcasted_iota(jnp.int32, (tl, HGRN_DK), 0) % c
    row = lax.broadcasted_iota(jnp.int32, (tl, tl), 0)
    col = lax.broadcasted_iota(jnp.int32, (tl, tl), 1)
    keep = (row // c == col // c) & (row >= col)

    vs, q_decs, decays, scores, kvts = [], [], [], [], []
    for hd in heads:
        cols = slice(hd * HGRN_DK, (hd + 1) * HGRN_DK)
        q = q_ref[0, :, cols]
        f = f_ref[0, :, cols]
        v = v_ref[0, :, cols]
        k = 1.0 - f
        b = _chunk_cumsum(jnp.log2(f), pos)

        def per_chunk(fn, b=b):
            return jnp.concatenate([jnp.broadcast_to(fn(b[rows, :]), (c, HGRN_DK)) for rows in chunks], axis=0)

        b_ref = per_chunk(lambda bc: bc[c // 2:c // 2 + 1])
        q_rel = q * jnp.exp2(b - b_ref)
        k_rel = k * jnp.exp2(b_ref - b)
        q_dec = q_rel * per_chunk(lambda bc: jnp.exp2(bc[c // 2:c // 2 + 1]))
        k_dec = (k_rel * per_chunk(lambda bc: jnp.exp2(bc[c - 1:c] - bc[c // 2:c // 2 + 1]))).astype(BF16)
        scores.append(_dot_nt(q_rel.astype(BF16), k_rel.astype(BF16)))
        kvts.append([_dot_tn(v[rows], k_dec[rows]) for rows in chunks])
        q_decs.append(q_dec.astype(BF16))
        decays.append([jnp.exp2(b[rows.stop - 1:rows.stop, :]) for rows in chunks])
        vs.append(v)

    for hd in heads:
        st = state_ref[hd]
        states = []
        for n in range(nsub):
            states.append(st.astype(BF16))
            st = st * decays[hd][n] + kvts[hd][n]
        state_ref[hd] = st
        o = _dot(jnp.where(keep, scores[hd], 0.0).astype(BF16), vs[hd])
        o = o + jnp.concatenate([_dot_nt(q_decs[hd][rows], states[n]) for n, rows in enumerate(chunks)], axis=0)
        cols = slice(hd * HGRN_DV, (hd + 1) * HGRN_DV)
        y = _rms(o, on_ref[...]) * gate_ref[0, :, cols].astype(F32)
        y_ref[0, :, cols] = y.astype(y_ref.dtype)


def _hgrn_rec(q, f, v, gate, o_gain):
    b, s, _ = q.shape
    nh = HGRN_STEP_HEADS
    blk = pl.BlockSpec((1, HGRN_TILE, nh * HGRN_DK), lambda bi, hg, i: (bi, i, hg))
    return pl.pallas_call(
        _hgrn_rec_kernel,
        grid=(b, HGRN_HEADS // nh, s // HGRN_TILE),
        in_specs=[blk, blk, blk, blk, _const_spec(o_gain.shape)],
        out_specs=blk,
        out_shape=jax.ShapeDtypeStruct((b, s, HGRN_HEADS * HGRN_DV), BF16),
        scratch_shapes=[pltpu.VMEM((nh, HGRN_DV, HGRN_DK), F32)],
        compiler_params=_params("parallel", "parallel", "arbitrary"),
        name="hgrn_recurrence",
    )(q, f, v, gate, o_gain)


def _post_mlp_kernel(h_ref, y_ref, wo_ref, g_ref, w1_ref, w2_ref, out_ref, u_ref):
    g = g_ref[...]
    tm = h_ref.shape[0]
    groups = [slice(k * tm // MLP_ROW_GROUPS, (k + 1) * tm // MLP_ROW_GROUPS) for k in range(MLP_ROW_GROUPS)]
    mixed = [_dot(y_ref[r, :], wo_ref[...]) for r in groups]
    h1 = [h_ref[r, :] + _rms(m, g[1:2]) for r, m in zip(groups, mixed)]
    a = [_rms(x, g[2:3]).astype(BF16) for x in h1]
    for c in range(D_FF // FF_CHUNK):
        cols = slice(c * FF_CHUNK, (c + 1) * FF_CHUNK)
        for r, ak in zip(groups, a):
            u = jnp.maximum(_dot(ak, w1_ref[:, cols]), 0.0)
            u_ref[r, cols] = (u * u).astype(BF16)
    down = [_dot(u_ref[r, :], w2_ref[...]) for r in groups]
    for r, x, d in zip(groups, h1, down):
        out_ref[r, :] = x + _rms(d, g[3:4])


def _post_mlp(h, y, w_o, slot, gains, w1, w2, layer):
    t = h.shape[0]
    tm = ROW_TILE
    row = pl.BlockSpec((tm, D_MODEL), lambda i: (i, 0))
    return pl.pallas_call(
        _post_mlp_kernel,
        grid=(t // tm,),
        in_specs=[row, row, _layer_spec(w_o, slot), _layer_spec(gains, layer), _layer_spec(w1, layer),
                  _layer_spec(w2, layer)],
        out_specs=row,
        out_shape=jax.ShapeDtypeStruct((t, D_MODEL), F32),
        scratch_shapes=[pltpu.VMEM((tm, D_FF), BF16)],
        compiler_params=_params("parallel"),
        name="post_mlp",
    )(h, y, w_o, gains, w1, w2)


def _swap_halves(w):
    half = MLA_ROPE // 2
    g = w.reshape(w.shape[0], -1, 2, half)
    return g[:, :, ::-1, :].reshape(w.shape)


def _mla_weights(w_in, w_uq, w_ukv):
    nq, nkv = MLA_Q_LORA, MLA_KV_LORA
    kr = w_in[:, nq + nkv:]
    krs = _swap_halves(kr)
    w_in_x = jnp.concatenate([w_in[:, :nq + nkv], kr, kr, krs, krs], axis=1).astype(BF16)
    uq = w_uq.reshape(nq, MLA_HEADS, MLA_NOPE + MLA_ROPE)
    uq_nope = uq[:, :, :MLA_NOPE].reshape(nq, -1)
    uq_rope = uq[:, :, MLA_NOPE:].reshape(nq, -1)
    w_uq_x = jnp.concatenate([uq_nope, uq_rope, _swap_halves(uq_rope)], axis=1).astype(BF16)
    ukv = w_ukv.reshape(nkv, MLA_HEADS, MLA_NOPE + MLA_V)
    w_ukv_x = jnp.concatenate([ukv[:, :, :MLA_NOPE].reshape(nkv, -1), ukv[:, :, MLA_NOPE:].reshape(nkv, -1)],
                              axis=1).astype(BF16)
    return w_in_x, w_uq_x, w_ukv_x


def kernel(x, positions, norm_gains, mla_w_in, mla_q_norm, mla_kv_norm, mla_w_uq, mla_w_ukv, mla_w_o, hgrn_w_in,
           hgrn_lb_logits, hgrn_o_norm, hgrn_w_o, mlp_w1, mlp_w2):
    bsz, seq, d = x.shape
    t = bsz * seq

    inv_freq = jnp.power(ROPE_BASE, -jnp.arange(0, MLA_ROPE, 2, dtype=F32) / MLA_ROPE)
    ang = positions.astype(F32).reshape(t, 1) * inv_freq
    ang = ang.reshape(-1, LANES)
    cos, sin = lax.optimization_barrier((jnp.cos(ang), jnp.sin(ang)))
    cos, sin = cos.reshape(t, -1), sin.reshape(t, -1)
    cos_t = jnp.concatenate([cos, cos, cos, cos], axis=1)
    sin_t = jnp.concatenate([-sin, sin, -sin, sin], axis=1)
    p = jax.nn.softmax(hgrn_lb_logits.astype(F32), axis=0)
    lower_bounds = jnp.cumsum(p, axis=0) - p[0]
    q_scale = (MLA_NOPE + MLA_ROPE) ** -0.5 * math.log2(math.e)

    mla_w_o, hgrn_w_in, hgrn_w_o, mlp_w1, mlp_w2 = (w.astype(BF16) for w in (mla_w_o, hgrn_w_in, hgrn_w_o, mlp_w1, mlp_w2))

    h = x.reshape(t, d)
    for layer in range(DEPTH):
        slot = layer // N_MIXERS
        if layer % N_MIXERS == 0:
            w_in, w_uq, w_ukv = _mla_weights(mla_w_in[slot], mla_w_uq[slot], mla_w_ukv[slot])
            qn, qr, kf, vt = _mla_proj(h, norm_gains[layer, 0:1], w_in, mla_q_norm[slot][None],
                                       mla_kv_norm[slot][None], w_uq, w_ukv, cos_t, sin_t, q_scale, bsz)
            y = _attention(qn.reshape(bsz, seq, -1), qr.reshape(bsz, seq, -1), kf,
                           vt.reshape(bsz, seq // ATTN_TK, -1, ATTN_TK))
            w_o = mla_w_o
        else:
            q, f, v, gate = _hgrn_proj(h, norm_gains[layer, 0:1], hgrn_w_in, slot, lower_bounds[layer][None])
            shp = (bsz, seq, -1)
            y = _hgrn_rec(q.reshape(shp), f.reshape(shp), v.reshape(shp), gate.reshape(shp),
                          hgrn_o_norm[slot][None])
            w_o = hgrn_w_o
        h = _post_mlp(h, y.reshape(t, -1), w_o, slot, norm_gains, mlp_w1, mlp_w2, layer)
    return h.reshape(bsz, seq, d)
```

```python
import functools
import math

import jax
import jax.numpy as jnp
from jax import lax
from jax.experimental import pallas as pl
from jax.experimental.pallas import tpu as pltpu

D_MODEL = 1024
DEPTH = 4
N_MIXERS = 2

MLA_HEADS = 8
MLA_Q_LORA = 512
MLA_KV_LORA = 256
MLA_NOPE = 128
MLA_ROPE = 64
MLA_V = 128
ROPE_BASE = 10000.0

HGRN_HEADS = 8
HGRN_DK = D_MODEL // HGRN_HEADS
HGRN_DV = D_MODEL // HGRN_HEADS
HGRN_CHUNK = 32

D_FF = 4 * D_MODEL
EPS = 1e-6

LANES = 128
VMEM_LIMIT = 56 * 1024 * 1024

ROW_TILE = 512
MLP_ROW_TILE = 1024
ATTN_TQ = 512
ATTN_TK = 256
ATTN_HEADS = 2
ATTN_UNROLL = 8
VT_ROWS = MLA_V + 16
HGRN_TILE = 256
HGRN_STEP_HEADS = 8
FF_CHUNK = 1024
MLP_ROW_GROUPS = 2

F32 = jnp.float32
BF16 = jnp.bfloat16


def _rms(x, g):
    return x * lax.rsqrt(jnp.mean(x * x, axis=-1, keepdims=True) + EPS) * g


def _dot(a, b):
    return jnp.dot(a, b, preferred_element_type=F32)


def _dot_nt(a, b):
    return lax.dot_general(a, b, (((1,), (1,)), ((), ())), preferred_element_type=F32)


def _dot_tn(a, b):
    return lax.dot_general(a, b, (((0,), (0,)), ((), ())), preferred_element_type=F32)


def _const_spec(shape):
    return pl.BlockSpec(shape, lambda *_: (0,) * len(shape))


def _layer_spec(stack, layer):
    return pl.BlockSpec((None,) + stack.shape[1:], lambda *_: (layer,) + (0,) * (stack.ndim - 1),
                        pipeline_mode=pl.Buffered(1))


def _params(*sem):
    return pltpu.CompilerParams(dimension_semantics=sem, vmem_limit_bytes=VMEM_LIMIT)


def _mla_proj_kernel(h_ref, g_ref, win_ref, qg_ref, kvg_ref, wuq_ref, wukv_ref, cos_ref, sin_ref,
                     qn_ref, qr_ref, kf_ref, vt_ref, *, q_scale):
    nq, nkv = MLA_Q_LORA, MLA_KV_LORA
    hn = MLA_HEADS * MLA_NOPE
    n_pair = MLA_HEADS * MLA_ROPE // LANES
    groups = [slice(c * ATTN_TK, (c + 1) * ATTN_TK) for c in range(vt_ref.shape[0])]
    a = [_rms(h_ref[r, :], g_ref[...]).astype(BF16) for r in groups]
    proj = [_dot(ak, win_ref[...]) for ak in a]
    c_q = [_rms(p[:, :nq], qg_ref[...]).astype(BF16) for p in proj]
    c_kv = [_rms(p[:, nq:nq + nkv], kvg_ref[...]).astype(BF16) for p in proj]
    q = [_dot(x, wuq_ref[...]) for x in c_q]
    kv = [_dot(x, wukv_ref[...]) for x in c_kv]
    ones = jnp.ones((VT_ROWS - MLA_V, ATTN_TK), BF16)
    lane = lax.broadcasted_iota(jnp.int32, (ATTN_TK, LANES), 1)
    for c, r in enumerate(groups):
        cos = cos_ref[r, :]
        sin = sin_ref[r, :]
        qn_ref[r, :] = (q[c][:, :hn] * q_scale).astype(BF16)
        for p in range(n_pair):
            qp = q[c][:, hn + p * LANES:hn + (p + 1) * LANES]
            qs = q[c][:, hn + (n_pair + p) * LANES:hn + (n_pair + p + 1) * LANES]
            qr_ref[r, p * LANES:(p + 1) * LANES] = ((qp * cos + qs * sin) * q_scale).astype(BF16)
        k_rope = proj[c][:, nq + nkv:nq + nkv + LANES] * cos + proj[c][:, nq + nkv + LANES:] * sin
        kr_even = jnp.where(lane < MLA_ROPE, k_rope, 0.0).astype(BF16)
        kr_odd = jnp.where(lane >= MLA_ROPE, k_rope, 0.0).astype(BF16)
        vt = kv[c][:, hn:].T.astype(BF16)
        for hd in range(MLA_HEADS):
            vt_ref[c, hd * VT_ROWS:hd * VT_ROWS + MLA_V, :] = vt[hd * MLA_V:(hd + 1) * MLA_V, :]
            vt_ref[c, hd * VT_ROWS + MLA_V:(hd + 1) * VT_ROWS, :] = ones
            off = (hd % 2) * 2 * LANES
            kf_ref[0, hd // 2, r, off:off + LANES] = kv[c][:, hd * LANES:(hd + 1) * LANES].astype(BF16)
            kf_ref[0, hd // 2, r, off + LANES:off + 2 * LANES] = kr_even if hd % 2 == 0 else kr_odd


def _mla_proj(h, gain, w_in, q_gain, kv_gain, w_uq, w_ukv, cos, sin, q_scale, bsz):
    t = h.shape[0]
    tm = ROW_TILE
    n_s = t // bsz // tm
    row = lambda w: pl.BlockSpec((tm, w), lambda i: (i, 0))
    hn = MLA_HEADS * MLA_NOPE
    return pl.pallas_call(
        functools.partial(_mla_proj_kernel, q_scale=q_scale),
        grid=(t // tm,),
        in_specs=[row(D_MODEL), _const_spec(gain.shape), _const_spec(w_in.shape), _const_spec(q_gain.shape),
                  _const_spec(kv_gain.shape), _const_spec(w_uq.shape), _const_spec(w_ukv.shape),
                  row(LANES), row(LANES)],
        out_specs=[row(hn), row(MLA_HEADS * MLA_ROPE),
                   pl.BlockSpec((1, MLA_HEADS // 2, tm, 4 * LANES), lambda i: (i // n_s, 0, i % n_s, 0)),
                   pl.BlockSpec((tm // ATTN_TK, MLA_HEADS * VT_ROWS, ATTN_TK), lambda i: (i, 0, 0))],
        out_shape=[jax.ShapeDtypeStruct((t, hn), BF16), jax.ShapeDtypeStruct((t, MLA_HEADS * MLA_ROPE), BF16),
                   jax.ShapeDtypeStruct((bsz, MLA_HEADS // 2, t // bsz, 4 * LANES), BF16),
                   jax.ShapeDtypeStruct((t // ATTN_TK, MLA_HEADS * VT_ROWS, ATTN_TK), BF16)],
        compiler_params=_params("parallel"),
        name="mla_proj",
    )(h, gain, w_in, q_gain, kv_gain, w_uq, w_ukv, cos, sin)


def _attn_kernel(qn_ref, qr_ref, k_ref, vt_ref, o_ref, qt_ref, sa_ref, sb_ref, pa_ref, pb_ref, aa_ref, ab_ref,
                 m_ref, l_ref, acc_ref):
    tq, tk = ATTN_TQ, ATTN_TK
    diag = tq // tk
    heads = range(ATTN_HEADS)

    def scores(j, st_ref):
        start = pl.multiple_of(j * tk, tk)
        for hd in heads:
            kc = (hd % 2) * 2 * LANES
            st_ref[hd] = _dot(k_ref[0, hd // 2, pl.ds(start, tk), kc:kc + 2 * LANES], qt_ref[hd])

    def softmax(st_ref, pt_ref, alpha_ref, mask_from=None):
        for hd in heads:
            st = st_ref[hd]
            if mask_from is not None:
                key = lax.broadcasted_iota(jnp.int32, st.shape, 0) + mask_from
                qry = lax.broadcasted_iota(jnp.int32, st.shape, 1)
                st = jnp.where(qry >= key, st, -jnp.inf)
            m = m_ref[hd]
            m_new = jnp.maximum(m, jnp.max(st, axis=0, keepdims=True))
            alpha = jnp.exp2(m - m_new)
            pt = jnp.exp2(st - m_new)
            m_ref[hd] = m_new
            pt_ref[hd] = pt.astype(BF16)
            alpha_ref[hd] = alpha

    def accumulate(j, pt_ref, alpha_ref):
        for hd in heads:
            pv = _dot(vt_ref[0, j, hd * VT_ROWS:(hd + 1) * VT_ROWS, :], pt_ref[hd])
            acc_ref[hd] = alpha_ref[hd] * acc_ref[hd] + pv[:MLA_V]
            l_ref[hd] = alpha_ref[hd] * l_ref[hd] + pv[MLA_V:MLA_V + 1]

    bufs = ((sa_ref, pa_ref, aa_ref), (sb_ref, pb_ref, ab_ref))

    def run(j0, n, last):
        for idx in range(n):
            (s_cur, p_cur, a_cur), (s_nxt, p_prv, a_prv) = bufs[idx % 2], bufs[1 - idx % 2]
            if not (last and idx == n - 1):
                scores(j0 + idx + 1, s_nxt)
            accumulate(jnp.maximum(j0 + idx - 1, 0), p_prv, a_prv)
            on_diag = last and idx >= n - diag
            softmax(s_cur, p_cur, a_cur, mask_from=(idx - (n - diag)) * tk if on_diag else None)
        if last:
            accumulate(j0 + n - 1, p_cur, a_cur)

    def group(g, carry):
        run(ATTN_UNROLL * g, ATTN_UNROLL, last=False)
        return carry

    n_tiles = qn_ref.shape[1] // tq

    def first_scores(i):
        rows = pl.ds(pl.multiple_of(i * tq, tq), tq)
        for hd in heads:
            qt_ref[hd, :LANES, :] = qn_ref[0, rows, hd * LANES:(hd + 1) * LANES].T
            qt_ref[hd, LANES:, :] = qr_ref[0, rows, (hd // 2) * LANES:(hd // 2 + 1) * LANES].T
        scores(0, sa_ref)

    def tile(i, carry):
        rows = pl.ds(pl.multiple_of(i * tq, tq), tq)
        m_ref[...] = jnp.full(m_ref.shape, -jnp.inf, F32)
        l_ref[...] = jnp.zeros(l_ref.shape, F32)
        acc_ref[...] = jnp.zeros(acc_ref.shape, F32)
        pb_ref[...] = jnp.zeros(pb_ref.shape, BF16)
        ab_ref[...] = jnp.ones(ab_ref.shape, F32)

        below = i * diag
        lax.fori_loop(0, below // ATTN_UNROLL, group, 0)
        for rest in range(0, ATTN_UNROLL, diag):
            @pl.when(below % ATTN_UNROLL == rest)
            def _(rest=rest):
                run(below - rest, rest + diag, last=True)

        first_scores(jnp.minimum(i + 1, n_tiles - 1))
        for hd in heads:
            o_ref[0, rows, hd * MLA_V:(hd + 1) * MLA_V] = (acc_ref[hd] / l_ref[hd]).T.astype(o_ref.dtype)
        return carry

    first_scores(0)
    lax.fori_loop(0, n_tiles, tile, 0)


def _attention(qn, qr, kf, vt):
    b, s, _ = qn.shape
    tq, tk = ATTN_TQ, ATTN_TK
    nh = ATTN_HEADS
    return pl.pallas_call(
        _attn_kernel,
        grid=(b, MLA_HEADS // nh),
        in_specs=[pl.BlockSpec((1, s, nh * LANES), lambda bi, hg: (bi, 0, hg)),
                  pl.BlockSpec((1, s, nh * LANES // 2), lambda bi, hg: (bi, 0, hg)),
                  pl.BlockSpec((1, nh // 2, s, 4 * LANES), lambda bi, hg: (bi, hg, 0, 0)),
                  pl.BlockSpec((1, s // tk, nh * VT_ROWS, tk), lambda bi, hg: (bi, 0, hg, 0))],
        out_specs=pl.BlockSpec((1, s, nh * MLA_V), lambda bi, hg: (bi, 0, hg)),
        out_shape=jax.ShapeDtypeStruct((b, s, MLA_HEADS * MLA_V), BF16),
        scratch_shapes=[pltpu.VMEM((nh, 2 * LANES, tq), BF16), pltpu.VMEM((nh, tk, tq), F32),
                        pltpu.VMEM((nh, tk, tq), F32), pltpu.VMEM((nh, tk, tq), BF16),
                        pltpu.VMEM((nh, tk, tq), BF16), pltpu.VMEM((nh, 1, tq), F32), pltpu.VMEM((nh, 1, tq), F32),
                        pltpu.VMEM((nh, 1, tq), F32), pltpu.VMEM((nh, 1, tq), F32),
                        pltpu.VMEM((nh, MLA_V, tq), F32)],
        compiler_params=_params("parallel", "parallel"),
        name="mla_attention",
    )(qn, qr, kf, vt)


def _hgrn_proj_kernel(h_ref, g_ref, win_ref, lb_ref, q_ref, f_ref, v_ref, gate_ref):
    hk = HGRN_HEADS * HGRN_DK
    tm = h_ref.shape[0]
    lb = lb_ref[...]
    groups = [slice(k * tm // MLP_ROW_GROUPS, (k + 1) * tm // MLP_ROW_GROUPS) for k in range(MLP_ROW_GROUPS)]
    a = [_rms(h_ref[r, :], g_ref[...]).astype(BF16) for r in groups]
    for r, ak in zip(groups, a):
        qx = _dot(ak, win_ref[:, :hk])
        q_ref[r, :] = qx * jax.nn.sigmoid(qx)
    for r, ak in zip(groups, a):
        fx = _dot(ak, win_ref[:, hk:2 * hk])
        f_ref[r, :] = lb + (1.0 - lb) * jax.nn.sigmoid(fx)
    for r, ak in zip(groups, a):
        v_ref[r, :] = _dot(ak, win_ref[:, 2 * hk:3 * hk]).astype(BF16)
    for r, ak in zip(groups, a):
        gx = _dot(ak, win_ref[:, 3 * hk:])
        gate_ref[r, :] = (gx * jax.nn.sigmoid(gx)).astype(BF16)


def _hgrn_proj(h, gain, w_in, slot, lb):
    t = h.shape[0]
    tm = ROW_TILE
    hk = HGRN_HEADS * HGRN_DK
    row = pl.BlockSpec((tm, hk), lambda i: (i, 0))
    return pl.pallas_call(
        _hgrn_proj_kernel,
        grid=(t // tm,),
        in_specs=[row, _const_spec(gain.shape), _layer_spec(w_in, slot), _const_spec(lb.shape)],
        out_specs=[row, row, row, row],
        out_shape=[jax.ShapeDtypeStruct((t, hk), F32), jax.ShapeDtypeStruct((t, hk), F32),
                   jax.ShapeDtypeStruct((t, hk), BF16), jax.ShapeDtypeStruct((t, hk), BF16)],
        compiler_params=_params("parallel"),
        name="hgrn_proj",
    )(h, gain, w_in, lb)


def _chunk_cumsum(x, pos):
    shift = 1
    while shift < HGRN_CHUNK:
        x = x + jnp.where(pos >= shift, pltpu.roll(x, shift, 0), 0.0)
        shift *= 2
    return x


def _hgrn_rec_kernel(q_ref, f_ref, v_ref, gate_ref, on_ref, y_ref, state_ref):
    c = HGRN_CHUNK
    tl = HGRN_TILE
    nsub = tl // c

    @pl.when(pl.program_id(2) == 0)
    def _():
        state_ref[...] = jnp.zeros_like(state_ref)

    heads = range(HGRN_STEP_HEADS)
    chunks = [slice(n * c, (n + 1) * c) for n in range(nsub)]
    pos = lax.broadcasted_iota(jnp.int32, (tl, HGRN_DK), 0) % c
    row = lax.broadcasted_iota(jnp.int32, (tl, tl), 0)
    col = lax.broadcasted_iota(jnp.int32, (tl, tl), 1)
    keep = (row // c == col // c) & (row >= col)

    vs, q_decs, decays, scores, kvts = [], [], [], [], []
    for hd in heads:
        cols = slice(hd * HGRN_DK, (hd + 1) * HGRN_DK)
        q = q_ref[0, :, cols]
        f = f_ref[0, :, cols]
        v = v_ref[0, :, cols]
        k = 1.0 - f
        b = _chunk_cumsum(jnp.log2(f), pos)

        def per_chunk(fn, b=b):
            return jnp.concatenate([jnp.broadcast_to(fn(b[rows, :]), (c, HGRN_DK)) for rows in chunks], axis=0)

        b_ref = per_chunk(lambda bc: bc[c // 2:c // 2 + 1])
        q_rel = q * jnp.exp2(b - b_ref)
        k_rel = k * jnp.exp2(b_ref - b)
        q_dec = q_rel * per_chunk(lambda bc: jnp.exp2(bc[c // 2:c // 2 + 1]))
        k_dec = (k_rel * per_chunk(lambda bc: jnp.exp2(bc[c - 1:c] - bc[c // 2:c // 2 + 1]))).astype(BF16)
        scores.append(_dot_nt(q_rel.astype(BF16), k_rel.astype(BF16)))
        kvts.append([_dot_tn(v[rows], k_dec[rows]) for rows in chunks])
        q_decs.append(q_dec.astype(BF16))
        decays.append([jnp.exp2(b[rows.stop - 1:rows.stop, :]) for rows in chunks])
        vs.append(v)

    for hd in heads:
        st = state_ref[hd]
        states = []
        for n in range(nsub):
            states.append(st.astype(BF16))
            st = st * decays[hd][n] + kvts[hd][n]
        state_ref[hd] = st
        o = _dot(jnp.where(keep, scores[hd], 0.0).astype(BF16), vs[hd])
        o = o + jnp.concatenate([_dot_nt(q_decs[hd][rows], states[n]) for n, rows in enumerate(chunks)], axis=0)
        cols = slice(hd * HGRN_DV, (hd + 1) * HGRN_DV)
        y = _rms(o, on_ref[...]) * gate_ref[0, :, cols].astype(F32)
        y_ref[0, :, cols] = y.astype(y_ref.dtype)


def _hgrn_rec(q, f, v, gate, o_gain):
    b, s, _ = q.shape
    nh = HGRN_STEP_HEADS
    blk = pl.BlockSpec((1, HGRN_TILE, nh * HGRN_DK), lambda bi, hg, i: (bi, i, hg))
    return pl.pallas_call(
        _hgrn_rec_kernel,
        grid=(b, HGRN_HEADS // nh, s // HGRN_TILE),
        in_specs=[blk, blk, blk, blk, _const_spec(o_gain.shape)],
        out_specs=blk,
        out_shape=jax.ShapeDtypeStruct((b, s, HGRN_HEADS * HGRN_DV), BF16),
        scratch_shapes=[pltpu.VMEM((nh, HGRN_DV, HGRN_DK), F32)],
        compiler_params=_params("parallel", "parallel", "arbitrary"),
        name="hgrn_recurrence",
    )(q, f, v, gate, o_gain)


def _post_mlp_kernel(h_ref, y_ref, wo_ref, g_ref, w1_ref, w2_ref, out_ref, u_ref):
    g = g_ref[...]
    tm = h_ref.shape[0]
    groups = [slice(k * tm // MLP_ROW_GROUPS, (k + 1) * tm // MLP_ROW_GROUPS) for k in range(MLP_ROW_GROUPS)]
    mixed = [_dot(y_ref[r, :], wo_ref[...]) for r in groups]
    h1 = [h_ref[r, :] + _rms(m, g[1:2]) for r, m in zip(groups, mixed)]
    a = [_rms(x, g[2:3]).astype(BF16) for x in h1]
    for c in range(D_FF // FF_CHUNK):
        cols = slice(c * FF_CHUNK, (c + 1) * FF_CHUNK)
        for r, ak in zip(groups, a):
            u = jnp.maximum(_dot(ak, w1_ref[:, cols]), 0.0)
            u_ref[r, cols] = (u * u).astype(BF16)
    down = [_dot(u_ref[r, :], w2_ref[...]) for r in groups]
    for r, x, d in zip(groups, h1, down):
        out_ref[r, :] = x + _rms(d, g[3:4])


def _post_mlp(h, y, w_o, slot, gains, w1, w2, layer):
    t = h.shape[0]
    tm = MLP_ROW_TILE
    row = pl.BlockSpec((tm, D_MODEL), lambda i: (i, 0))
    return pl.pallas_call(
        _post_mlp_kernel,
        grid=(t // tm,),
        in_specs=[row, row, _layer_spec(w_o, slot), _layer_spec(gains, layer), _layer_spec(w1, layer),
                  _layer_spec(w2, layer)],
        out_specs=row,
        out_shape=jax.ShapeDtypeStruct((t, D_MODEL), F32),
        scratch_shapes=[pltpu.VMEM((tm, D_FF), BF16)],
        compiler_params=_params("parallel"),
        name="post_mlp",
    )(h, y, w_o, gains, w1, w2)


def _swap_halves(w):
    half = MLA_ROPE // 2
    g = w.reshape(w.shape[0], -1, 2, half)
    return g[:, :, ::-1, :].reshape(w.shape)


def _mla_weights(w_in, w_uq, w_ukv):
    nq, nkv = MLA_Q_LORA, MLA_KV_LORA
    kr = w_in[:, nq + nkv:]
    krs = _swap_halves(kr)
    w_in_x = jnp.concatenate([w_in[:, :nq + nkv], kr, kr, krs, krs], axis=1).astype(BF16)
    uq = w_uq.reshape(nq, MLA_HEADS, MLA_NOPE + MLA_ROPE)
    uq_nope = uq[:, :, :MLA_NOPE].reshape(nq, -1)
    uq_rope = uq[:, :, MLA_NOPE:].reshape(nq, -1)
    w_uq_x = jnp.concatenate([uq_nope, uq_rope, _swap_halves(uq_rope)], axis=1).astype(BF16)
    ukv = w_ukv.reshape(nkv, MLA_HEADS, MLA_NOPE + MLA_V)
    w_ukv_x = jnp.concatenate([ukv[:, :, :MLA_NOPE].reshape(nkv, -1), ukv[:, :, MLA_NOPE:].reshape(nkv, -1)],
                              axis=1).astype(BF16)
    return w_in_x, w_uq_x, w_ukv_x


def kernel(x, positions, norm_gains, mla_w_in, mla_q_norm, mla_kv_norm, mla_w_uq, mla_w_ukv, mla_w_o, hgrn_w_in,
           hgrn_lb_logits, hgrn_o_norm, hgrn_w_o, mlp_w1, mlp_w2):
    bsz, seq, d = x.shape
    t = bsz * seq

    inv_freq = jnp.power(ROPE_BASE, -jnp.arange(0, MLA_ROPE, 2, dtype=F32) / MLA_ROPE)
    ang = positions.astype(F32).reshape(t, 1) * inv_freq
    ang = ang.reshape(-1, LANES)
    cos, sin = lax.optimization_barrier((jnp.cos(ang), jnp.sin(ang)))
    cos, sin = cos.reshape(t, -1), sin.reshape(t, -1)
    half_sign = jnp.repeat(jnp.array([-1.0, 1.0, -1.0, 1.0], F32), MLA_ROPE // 2)
    cos_t = jnp.tile(cos, (1, 4))
    sin_t = jnp.tile(sin, (1, 4)) * half_sign
    p = jax.nn.softmax(hgrn_lb_logits.astype(F32), axis=0)
    lower_bounds = jnp.cumsum(p, axis=0) - p[0]
    q_scale = (MLA_NOPE + MLA_ROPE) ** -0.5 * math.log2(math.e)

    mla_w_o, hgrn_w_in, hgrn_w_o, mlp_w1, mlp_w2 = (w.astype(BF16) for w in (mla_w_o, hgrn_w_in, hgrn_w_o, mlp_w1, mlp_w2))

    h = x.reshape(t, d)
    for layer in range(DEPTH):
        slot = layer // N_MIXERS
        if layer % N_MIXERS == 0:
            w_in, w_uq, w_ukv = _mla_weights(mla_w_in[slot], mla_w_uq[slot], mla_w_ukv[slot])
            qn, qr, kf, vt = _mla_proj(h, norm_gains[layer, 0:1], w_in, mla_q_norm[slot][None],
                                       mla_kv_norm[slot][None], w_uq, w_ukv, cos_t, sin_t, q_scale, bsz)
            y = _attention(qn.reshape(bsz, seq, -1), qr.reshape(bsz, seq, -1), kf,
                           vt.reshape(bsz, seq // ATTN_TK, -1, ATTN_TK))
            w_o = mla_w_o
        else:
            q, f, v, gate = _hgrn_proj(h, norm_gains[layer, 0:1], hgrn_w_in, slot, lower_bounds[layer][None])
            shp = (bsz, seq, -1)
            y = _hgrn_rec(q.reshape(shp), f.reshape(shp), v.reshape(shp), gate.reshape(shp),
                          hgrn_o_norm[slot][None])
            w_o = hgrn_w_o
        h = _post_mlp(h, y.reshape(t, -1), w_o, slot, norm_gains, mlp_w1, mlp_w2, layer)
    return h.reshape(bsz, seq, d)
```

```python
import functools
import math

import jax
import jax.numpy as jnp
from jax import lax
from jax.experimental import pallas as pl
from jax.experimental.pallas import tpu as pltpu

D_MODEL = 1024
DEPTH = 4
N_MIXERS = 2

MLA_HEADS = 8
MLA_Q_LORA = 512
MLA_KV_LORA = 256
MLA_NOPE = 128
MLA_ROPE = 64
MLA_V = 128
ROPE_BASE = 10000.0

HGRN_HEADS = 8
HGRN_DK = D_MODEL // HGRN_HEADS
HGRN_DV = D_MODEL // HGRN_HEADS
HGRN_CHUNK = 32

D_FF = 4 * D_MODEL
EPS = 1e-6

LANES = 128
VMEM_LIMIT = 56 * 1024 * 1024

ROW_TILE = 512
MLP_ROW_TILE = 1024
ATTN_TQ = 512
ATTN_TK = 256
ATTN_HEADS = 2
ATTN_UNROLL = 8
VT_ROWS = MLA_V + 16
HGRN_TILE = 256
HGRN_STEP_HEADS = 8
FF_CHUNK = 1024
MLP_ROW_GROUPS = 2

F32 = jnp.float32
BF16 = jnp.bfloat16


def _rms(x, g):
    return x * lax.rsqrt(jnp.mean(x * x, axis=-1, keepdims=True) + EPS) * g


def _dot(a, b):
    return jnp.dot(a, b, preferred_element_type=F32)


def _dot_nt(a, b):
    return lax.dot_general(a, b, (((1,), (1,)), ((), ())), preferred_element_type=F32)


def _dot_tn(a, b):
    return lax.dot_general(a, b, (((0,), (0,)), ((), ())), preferred_element_type=F32)


def _const_spec(shape):
    return pl.BlockSpec(shape, lambda *_: (0,) * len(shape))


def _layer_spec(stack, layer):
    return pl.BlockSpec((None,) + stack.shape[1:], lambda *_: (layer,) + (0,) * (stack.ndim - 1),
                        pipeline_mode=pl.Buffered(1))


def _params(*sem):
    return pltpu.CompilerParams(dimension_semantics=sem, vmem_limit_bytes=VMEM_LIMIT)


def _mla_proj_kernel(h_ref, g_ref, win_ref, qg_ref, kvg_ref, wuq_ref, wukv_ref, cos_ref, sin_ref,
                     qn_ref, qr_ref, kf_ref, vt_ref, *, q_scale):
    nq, nkv = MLA_Q_LORA, MLA_KV_LORA
    hn = MLA_HEADS * MLA_NOPE
    n_pair = MLA_HEADS * MLA_ROPE // LANES
    groups = [slice(c * ATTN_TK, (c + 1) * ATTN_TK) for c in range(vt_ref.shape[0])]
    a = [_rms(h_ref[r, :], g_ref[...]).astype(BF16) for r in groups]
    proj = [_dot(ak, win_ref[...]) for ak in a]
    c_q = [_rms(p[:, :nq], qg_ref[...]).astype(BF16) for p in proj]
    c_kv = [_rms(p[:, nq:nq + nkv], kvg_ref[...]).astype(BF16) for p in proj]
    q = [_dot(x, wuq_ref[...]) for x in c_q]
    kv = [_dot(x, wukv_ref[...]) for x in c_kv]
    ones = jnp.ones((VT_ROWS - MLA_V, ATTN_TK), BF16)
    lane = lax.broadcasted_iota(jnp.int32, (ATTN_TK, LANES), 1)
    first_half = lane % MLA_ROPE < MLA_ROPE // 2

    def rope(x, cos, sin):
        swapped = jnp.where(first_half, pltpu.roll(x, LANES - MLA_ROPE // 2, 1), pltpu.roll(x, MLA_ROPE // 2, 1))
        return x * cos + swapped * sin

    for c, r in enumerate(groups):
        cos = cos_ref[r, :]
        sin = sin_ref[r, :]
        qn_ref[r, :] = (q[c][:, :hn] * q_scale).astype(BF16)
        for p in range(n_pair):
            qp = q[c][:, hn + p * LANES:hn + (p + 1) * LANES]
            qr_ref[r, p * LANES:(p + 1) * LANES] = (rope(qp, cos, sin) * q_scale).astype(BF16)
        k_rope = rope(proj[c][:, nq + nkv:], cos, sin)
        kr_even = jnp.where(lane < MLA_ROPE, k_rope, 0.0).astype(BF16)
        kr_odd = jnp.where(lane >= MLA_ROPE, k_rope, 0.0).astype(BF16)
        vt = kv[c][:, hn:].T.astype(BF16)
        for hd in range(MLA_HEADS):
            vt_ref[c, hd * VT_ROWS:hd * VT_ROWS + MLA_V, :] = vt[hd * MLA_V:(hd + 1) * MLA_V, :]
            vt_ref[c, hd * VT_ROWS + MLA_V:(hd + 1) * VT_ROWS, :] = ones
            off = (hd % 2) * 2 * LANES
            kf_ref[0, hd // 2, r, off:off + LANES] = kv[c][:, hd * LANES:(hd + 1) * LANES].astype(BF16)
            kf_ref[0, hd // 2, r, off + LANES:off + 2 * LANES] = kr_even if hd % 2 == 0 else kr_odd


def _mla_proj(h, gain, w_in, q_gain, kv_gain, w_uq, w_ukv, slot, cos, sin, q_scale, bsz):
    t = h.shape[0]
    tm = ROW_TILE
    n_s = t // bsz // tm
    row = lambda w: pl.BlockSpec((tm, w), lambda i: (i, 0))
    hn = MLA_HEADS * MLA_NOPE
    return pl.pallas_call(
        functools.partial(_mla_proj_kernel, q_scale=q_scale),
        grid=(t // tm,),
        in_specs=[row(D_MODEL), _const_spec(gain.shape), _layer_spec(w_in, slot), _const_spec(q_gain.shape),
                  _const_spec(kv_gain.shape), _layer_spec(w_uq, slot), _layer_spec(w_ukv, slot),
                  row(LANES), row(LANES)],
        out_specs=[row(hn), row(MLA_HEADS * MLA_ROPE),
                   pl.BlockSpec((1, MLA_HEADS // 2, tm, 4 * LANES), lambda i: (i // n_s, 0, i % n_s, 0)),
                   pl.BlockSpec((tm // ATTN_TK, MLA_HEADS * VT_ROWS, ATTN_TK), lambda i: (i, 0, 0))],
        out_shape=[jax.ShapeDtypeStruct((t, hn), BF16), jax.ShapeDtypeStruct((t, MLA_HEADS * MLA_ROPE), BF16),
                   jax.ShapeDtypeStruct((bsz, MLA_HEADS // 2, t // bsz, 4 * LANES), BF16),
                   jax.ShapeDtypeStruct((t // ATTN_TK, MLA_HEADS * VT_ROWS, ATTN_TK), BF16)],
        compiler_params=_params("parallel"),
        name="mla_proj",
    )(h, gain, w_in, q_gain, kv_gain, w_uq, w_ukv, cos, sin)


def _attn_kernel(qn_ref, qr_ref, k_ref, vt_ref, o_ref, qt_ref, sa_ref, sb_ref, pa_ref, pb_ref, aa_ref, ab_ref,
                 m_ref, l_ref, acc_ref):
    tq, tk = ATTN_TQ, ATTN_TK
    diag = tq // tk
    heads = range(ATTN_HEADS)

    def scores(j, st_ref):
        start = pl.multiple_of(j * tk, tk)
        for hd in heads:
            kc = (hd % 2) * 2 * LANES
            st_ref[hd] = _dot(k_ref[0, hd // 2, pl.ds(start, tk), kc:kc + 2 * LANES], qt_ref[hd])

    def softmax(st_ref, pt_ref, alpha_ref, mask_from=None):
        for hd in heads:
            st = st_ref[hd]
            if mask_from is not None:
                key = lax.broadcasted_iota(jnp.int32, st.shape, 0) + mask_from
                qry = lax.broadcasted_iota(jnp.int32, st.shape, 1)
                st = jnp.where(qry >= key, st, -jnp.inf)
            m = m_ref[hd]
            m_new = jnp.maximum(m, jnp.max(st, axis=0, keepdims=True))
            alpha = jnp.exp2(m - m_new)
            pt = jnp.exp2(st - m_new)
            m_ref[hd] = m_new
            pt_ref[hd] = pt.astype(BF16)
            alpha_ref[hd] = alpha

    def accumulate(j, pt_ref, alpha_ref):
        for hd in heads:
            pv = _dot(vt_ref[0, j, hd * VT_ROWS:(hd + 1) * VT_ROWS, :], pt_ref[hd])
            acc_ref[hd] = alpha_ref[hd] * acc_ref[hd] + pv[:MLA_V]
            l_ref[hd] = alpha_ref[hd] * l_ref[hd] + pv[MLA_V:MLA_V + 1]

    bufs = ((sa_ref, pa_ref, aa_ref), (sb_ref, pb_ref, ab_ref))

    def run(j0, n, last):
        for idx in range(n):
            (s_cur, p_cur, a_cur), (s_nxt, p_prv, a_prv) = bufs[idx % 2], bufs[1 - idx % 2]
            if not (last and idx == n - 1):
                scores(j0 + idx + 1, s_nxt)
            accumulate(jnp.maximum(j0 + idx - 1, 0), p_prv, a_prv)
            on_diag = last and idx >= n - diag
            softmax(s_cur, p_cur, a_cur, mask_from=(idx - (n - diag)) * tk if on_diag else None)
        if last:
            accumulate(j0 + n - 1, p_cur, a_cur)

    def group(g, carry):
        run(ATTN_UNROLL * g, ATTN_UNROLL, last=False)
        return carry

    n_tiles = qn_ref.shape[1] // tq

    def first_scores(i):
        rows = pl.ds(pl.multiple_of(i * tq, tq), tq)
        for hd in heads:
            qt_ref[hd, :LANES, :] = qn_ref[0, rows, hd * LANES:(hd + 1) * LANES].T
            qt_ref[hd, LANES:, :] = qr_ref[0, rows, (hd // 2) * LANES:(hd // 2 + 1) * LANES].T
        scores(0, sa_ref)

    def tile(i, carry):
        rows = pl.ds(pl.multiple_of(i * tq, tq), tq)
        m_ref[...] = jnp.full(m_ref.shape, -jnp.inf, F32)
        l_ref[...] = jnp.zeros(l_ref.shape, F32)
        acc_ref[...] = jnp.zeros(acc_ref.shape, F32)
        pb_ref[...] = jnp.zeros(pb_ref.shape, BF16)
        ab_ref[...] = jnp.ones(ab_ref.shape, F32)

        below = i * diag
        lax.fori_loop(0, below // ATTN_UNROLL, group, 0)
        for rest in range(0, ATTN_UNROLL, diag):
            @pl.when(below % ATTN_UNROLL == rest)
            def _(rest=rest):
                run(below - rest, rest + diag, last=True)

        first_scores(jnp.minimum(i + 1, n_tiles - 1))
        for hd in heads:
            o_ref[0, rows, hd * MLA_V:(hd + 1) * MLA_V] = (acc_ref[hd] / l_ref[hd]).T.astype(o_ref.dtype)
        return carry

    first_scores(0)
    lax.fori_loop(0, n_tiles, tile, 0)


def _attention(qn, qr, kf, vt):
    b, s, _ = qn.shape
    tq, tk = ATTN_TQ, ATTN_TK
    nh = ATTN_HEADS
    return pl.pallas_call(
        _attn_kernel,
        grid=(b, MLA_HEADS // nh),
        in_specs=[pl.BlockSpec((1, s, nh * LANES), lambda bi, hg: (bi, 0, hg)),
                  pl.BlockSpec((1, s, nh * LANES // 2), lambda bi, hg: (bi, 0, hg)),
                  pl.BlockSpec((1, nh // 2, s, 4 * LANES), lambda bi, hg: (bi, hg, 0, 0)),
                  pl.BlockSpec((1, s // tk, nh * VT_ROWS, tk), lambda bi, hg: (bi, 0, hg, 0))],
        out_specs=pl.BlockSpec((1, s, nh * MLA_V), lambda bi, hg: (bi, 0, hg)),
        out_shape=jax.ShapeDtypeStruct((b, s, MLA_HEADS * MLA_V), BF16),
        scratch_shapes=[pltpu.VMEM((nh, 2 * LANES, tq), BF16), pltpu.VMEM((nh, tk, tq), F32),
                        pltpu.VMEM((nh, tk, tq), F32), pltpu.VMEM((nh, tk, tq), BF16),
                        pltpu.VMEM((nh, tk, tq), BF16), pltpu.VMEM((nh, 1, tq), F32), pltpu.VMEM((nh, 1, tq), F32),
                        pltpu.VMEM((nh, 1, tq), F32), pltpu.VMEM((nh, 1, tq), F32),
                        pltpu.VMEM((nh, MLA_V, tq), F32)],
        compiler_params=_params("parallel", "parallel"),
        name="mla_attention",
    )(qn, qr, kf, vt)


def _hgrn_proj_kernel(h_ref, g_ref, win_ref, lb_ref, q_ref, f_ref, v_ref, gate_ref):
    hk = HGRN_HEADS * HGRN_DK
    tm = h_ref.shape[0]
    lb = lb_ref[...]
    groups = [slice(k * tm // MLP_ROW_GROUPS, (k + 1) * tm // MLP_ROW_GROUPS) for k in range(MLP_ROW_GROUPS)]
    a = [_rms(h_ref[r, :], g_ref[...]).astype(BF16) for r in groups]
    for r, ak in zip(groups, a):
        qx = _dot(ak, win_ref[:, :hk])
        q_ref[r, :] = qx * jax.nn.sigmoid(qx)
    for r, ak in zip(groups, a):
        fx = _dot(ak, win_ref[:, hk:2 * hk])
        f_ref[r, :] = lb + (1.0 - lb) * jax.nn.sigmoid(fx)
    for r, ak in zip(groups, a):
        v_ref[r, :] = _dot(ak, win_ref[:, 2 * hk:3 * hk]).astype(BF16)
    for r, ak in zip(groups, a):
        gx = _dot(ak, win_ref[:, 3 * hk:])
        gate_ref[r, :] = (gx * jax.nn.sigmoid(gx)).astype(BF16)


def _hgrn_proj(h, gain, w_in, slot, lb):
    t = h.shape[0]
    tm = ROW_TILE
    hk = HGRN_HEADS * HGRN_DK
    row = pl.BlockSpec((tm, hk), lambda i: (i, 0))
    return pl.pallas_call(
        _hgrn_proj_kernel,
        grid=(t // tm,),
        in_specs=[row, _const_spec(gain.shape), _layer_spec(w_in, slot), _const_spec(lb.shape)],
        out_specs=[row, row, row, row],
        out_shape=[jax.ShapeDtypeStruct((t, hk), F32), jax.ShapeDtypeStruct((t, hk), F32),
                   jax.ShapeDtypeStruct((t, hk), BF16), jax.ShapeDtypeStruct((t, hk), BF16)],
        compiler_params=_params("parallel"),
        name="hgrn_proj",
    )(h, gain, w_in, lb)


def _chunk_cumsum(x, pos):
    shift = 1
    while shift < HGRN_CHUNK:
        x = x + jnp.where(pos >= shift, pltpu.roll(x, shift, 0), 0.0)
        shift *= 2
    return x


def _hgrn_rec_kernel(q_ref, f_ref, v_ref, gate_ref, on_ref, y_ref, state_ref):
    c = HGRN_CHUNK
    tl = HGRN_TILE
    nsub = tl // c

    @pl.when(pl.program_id(2) == 0)
    def _():
        state_ref[...] = jnp.zeros_like(state_ref)

    heads = range(HGRN_STEP_HEADS)
    chunks = [slice(n * c, (n + 1) * c) for n in range(nsub)]
    pos = lax.broadcasted_iota(jnp.int32, (tl, HGRN_DK), 0) % c
    row = lax.broadcasted_iota(jnp.int32, (tl, tl), 0)
    col = lax.broadcasted_iota(jnp.int32, (tl, tl), 1)
    keep = (row // c == col // c) & (row >= col)

    vs, q_decs, decays, scores, kvts = [], [], [], [], []
    for hd in heads:
        cols = slice(hd * HGRN_DK, (hd + 1) * HGRN_DK)
        q = q_ref[0, :, cols]
        f = f_ref[0, :, cols]
        v = v_ref[0, :, cols]
        k = 1.0 - f
        b = _chunk_cumsum(jnp.log2(f), pos)

        def per_chunk(fn, b=b):
            return jnp.concatenate([jnp.broadcast_to(fn(b[rows, :]), (c, HGRN_DK)) for rows in chunks], axis=0)

        b_ref = per_chunk(lambda bc: bc[c // 2:c // 2 + 1])
        q_rel = q * jnp.exp2(b - b_ref)
        k_rel = k * jnp.exp2(b_ref - b)
        q_dec = q_rel * per_chunk(lambda bc: jnp.exp2(bc[c // 2:c // 2 + 1]))
        k_dec = (k_rel * per_chunk(lambda bc: jnp.exp2(bc[c - 1:c] - bc[c // 2:c // 2 + 1]))).astype(BF16)
        scores.append(_dot_nt(q_rel.astype(BF16), k_rel.astype(BF16)))
        kvts.append([_dot_tn(v[rows], k_dec[rows]) for rows in chunks])
        q_decs.append(q_dec.astype(BF16))
        decays.append([jnp.exp2(b[rows.stop - 1:rows.stop, :]) for rows in chunks])
        vs.append(v)

    for hd in heads:
        st = state_ref[hd]
        states = []
        for n in range(nsub):
            states.append(st.astype(BF16))
            st = st * decays[hd][n] + kvts[hd][n]
        state_ref[hd] = st
        o = _dot(jnp.where(keep, scores[hd], 0.0).astype(BF16), vs[hd])
        o = o + jnp.concatenate([_dot_nt(q_decs[hd][rows], states[n]) for n, rows in enumerate(chunks)], axis=0)
        cols = slice(hd * HGRN_DV, (hd + 1) * HGRN_DV)
        y = _rms(o, on_ref[...]) * gate_ref[0, :, cols].astype(F32)
        y_ref[0, :, cols] = y.astype(y_ref.dtype)


def _hgrn_rec(q, f, v, gate, o_gain):
    b, s, _ = q.shape
    nh = HGRN_STEP_HEADS
    blk = pl.BlockSpec((1, HGRN_TILE, nh * HGRN_DK), lambda bi, hg, i: (bi, i, hg))
    return pl.pallas_call(
        _hgrn_rec_kernel,
        grid=(b, HGRN_HEADS // nh, s // HGRN_TILE),
        in_specs=[blk, blk, blk, blk, _const_spec(o_gain.shape)],
        out_specs=blk,
        out_shape=jax.ShapeDtypeStruct((b, s, HGRN_HEADS * HGRN_DV), BF16),
        scratch_shapes=[pltpu.VMEM((nh, HGRN_DV, HGRN_DK), F32)],
        compiler_params=_params("parallel", "parallel", "arbitrary"),
        name="hgrn_recurrence",
    )(q, f, v, gate, o_gain)


def _post_mlp_kernel(h_ref, y_ref, wo_ref, g_ref, w1_ref, w2_ref, out_ref, u_ref):
    g = g_ref[...]
    tm = h_ref.shape[0]
    groups = [slice(k * tm // MLP_ROW_GROUPS, (k + 1) * tm // MLP_ROW_GROUPS) for k in range(MLP_ROW_GROUPS)]
    mixed = [_dot(y_ref[r, :], wo_ref[...]) for r in groups]
    h1 = [h_ref[r, :] + _rms(m, g[1:2]) for r, m in zip(groups, mixed)]
    a = [_rms(x, g[2:3]).astype(BF16) for x in h1]
    for c in range(D_FF // FF_CHUNK):
        cols = slice(c * FF_CHUNK, (c + 1) * FF_CHUNK)
        for r, ak in zip(groups, a):
            u = jnp.maximum(_dot(ak, w1_ref[:, cols]), 0.0)
            u_ref[r, cols] = (u * u).astype(BF16)
    down = [_dot(u_ref[r, :], w2_ref[...]) for r in groups]
    for r, x, d in zip(groups, h1, down):
        out_ref[r, :] = x + _rms(d, g[3:4])


def _post_mlp(h, y, w_o, slot, gains, w1, w2, layer):
    t = h.shape[0]
    tm = MLP_ROW_TILE
    row = pl.BlockSpec((tm, D_MODEL), lambda i: (i, 0))
    return pl.pallas_call(
        _post_mlp_kernel,
        grid=(t // tm,),
        in_specs=[row, row, _layer_spec(w_o, slot), _layer_spec(gains, layer), _layer_spec(w1, layer),
                  _layer_spec(w2, layer)],
        out_specs=row,
        out_shape=jax.ShapeDtypeStruct((t, D_MODEL), F32),
        scratch_shapes=[pltpu.VMEM((tm, D_FF), BF16)],
        compiler_params=_params("parallel"),
        name="post_mlp",
    )(h, y, w_o, gains, w1, w2)


def _mla_weights(w_in, w_uq, w_ukv):
    n, nq, nkv = w_in.shape[0], MLA_Q_LORA, MLA_KV_LORA
    kr = w_in[:, :, nq + nkv:]
    w_in_x = jnp.concatenate([w_in[:, :, :nq + nkv], kr, kr], axis=2).astype(BF16)
    uq = w_uq.reshape(n, nq, MLA_HEADS, MLA_NOPE + MLA_ROPE)
    w_uq_x = jnp.concatenate([uq[..., :MLA_NOPE].reshape(n, nq, -1), uq[..., MLA_NOPE:].reshape(n, nq, -1)],
                             axis=2).astype(BF16)
    ukv = w_ukv.reshape(n, nkv, MLA_HEADS, MLA_NOPE + MLA_V)
    w_ukv_x = jnp.concatenate([ukv[..., :MLA_NOPE].reshape(n, nkv, -1), ukv[..., MLA_NOPE:].reshape(n, nkv, -1)],
                              axis=2).astype(BF16)
    return w_in_x, w_uq_x, w_ukv_x


def kernel(x, positions, norm_gains, mla_w_in, mla_q_norm, mla_kv_norm, mla_w_uq, mla_w_ukv, mla_w_o, hgrn_w_in,
           hgrn_lb_logits, hgrn_o_norm, hgrn_w_o, mlp_w1, mlp_w2):
    bsz, seq, d = x.shape
    t = bsz * seq

    inv_freq = jnp.power(ROPE_BASE, -jnp.arange(0, MLA_ROPE, 2, dtype=F32) / MLA_ROPE)
    ang = positions.astype(F32).reshape(t, 1) * inv_freq
    ang = ang.reshape(-1, LANES)
    cos, sin = lax.optimization_barrier((jnp.cos(ang), jnp.sin(ang)))
    cos, sin = cos.reshape(t, -1), sin.reshape(t, -1)
    cos_t = jnp.concatenate([cos, cos, cos, cos], axis=1)
    sin_t = jnp.concatenate([-sin, sin, -sin, sin], axis=1)
    p = jax.nn.softmax(hgrn_lb_logits.astype(F32), axis=0)
    lower_bounds = jnp.cumsum(p, axis=0) - p[0]
    q_scale = (MLA_NOPE + MLA_ROPE) ** -0.5 * math.log2(math.e)

    mla_w_o, hgrn_w_in, hgrn_w_o, mlp_w1, mlp_w2 = (w.astype(BF16) for w in (mla_w_o, hgrn_w_in, hgrn_w_o, mlp_w1, mlp_w2))
    mla_w_in, mla_w_uq, mla_w_ukv = _mla_weights(mla_w_in, mla_w_uq, mla_w_ukv)

    h = x.reshape(t, d)
    for layer in range(DEPTH):
        slot = layer // N_MIXERS
        if layer % N_MIXERS == 0:
            qn, qr, kf, vt = _mla_proj(h, norm_gains[layer, 0:1], mla_w_in, mla_q_norm[slot][None],
                                       mla_kv_norm[slot][None], mla_w_uq, mla_w_ukv, slot, cos_t, sin_t, q_scale, bsz)
            y = _attention(qn.reshape(bsz, seq, -1), qr.reshape(bsz, seq, -1), kf,
                           vt.reshape(bsz, seq // ATTN_TK, -1, ATTN_TK))
            w_o = mla_w_o
        else:
            q, f, v, gate = _hgrn_proj(h, norm_gains[layer, 0:1], hgrn_w_in, slot, lower_bounds[layer][None])
            shp = (bsz, seq, -1)
            y = _hgrn_rec(q.reshape(shp), f.reshape(shp), v.reshape(shp), gate.reshape(shp),
                          hgrn_o_norm[slot][None])
            w_o = hgrn_w_o
        h = _post_mlp(h, y.reshape(t, -1), w_o, slot, norm_gains, mlp_w1, mlp_w2, layer)
    return h.reshape(bsz, seq, d)
```

```python
import functools
import math

import jax
import jax.numpy as jnp
from jax import lax
from jax.experimental import pallas as pl
from jax.experimental.pallas import tpu as pltpu

D_MODEL = 1024
DEPTH = 4
N_MIXERS = 2

MLA_HEADS = 8
MLA_Q_LORA = 512
MLA_KV_LORA = 256
MLA_NOPE = 128
MLA_ROPE = 64
MLA_V = 128
ROPE_BASE = 10000.0

HGRN_HEADS = 8
HGRN_DK = D_MODEL // HGRN_HEADS
HGRN_DV = D_MODEL // HGRN_HEADS
HGRN_CHUNK = 32

D_FF = 4 * D_MODEL
EPS = 1e-6

LANES = 128
BF16_SUBLANES = 16
VMEM_LIMIT = 56 * 1024 * 1024

ROW_TILE = 512
MLP_ROW_TILE = 1024
ROW_GROUPS = 2
FF_CHUNK = 1024
ATTN_TQ = 512
ATTN_TK = 256
ATTN_HEADS = 2
ATTN_UNROLL = 8
VT_ROWS = MLA_V + BF16_SUBLANES
KEY_WIDTH = MLA_NOPE + LANES
KEY_PAIR_WIDTH = 2 * KEY_WIDTH
HGRN_TILE = 256
HGRN_STEP_HEADS = 8

F32 = jnp.float32
BF16 = jnp.bfloat16


def _rms(x, g):
    return x * lax.rsqrt(jnp.mean(x * x, axis=-1, keepdims=True) + EPS) * g


def _dot(a, b):
    return jnp.dot(a, b, preferred_element_type=F32)


def _dot_nt(a, b):
    return lax.dot_general(a, b, (((1,), (1,)), ((), ())), preferred_element_type=F32)


def _dot_tn(a, b):
    return lax.dot_general(a, b, (((0,), (0,)), ((), ())), preferred_element_type=F32)


def _const_spec(shape):
    return pl.BlockSpec(shape, lambda *_: (0,) * len(shape))


def _layer_spec(stack, layer):
    return pl.BlockSpec((None,) + stack.shape[1:], lambda *_: (layer,) + (0,) * (stack.ndim - 1),
                        pipeline_mode=pl.Buffered(1))


def _params(*sem):
    return pltpu.CompilerParams(dimension_semantics=sem, vmem_limit_bytes=VMEM_LIMIT)


def _mla_proj_kernel(h_ref, g_ref, win_ref, qg_ref, kvg_ref, wuq_ref, wukv_ref, cos_ref, sin_ref,
                     qn_ref, qr_ref, kf_ref, vt_ref, *, q_scale):
    nq, nkv = MLA_Q_LORA, MLA_KV_LORA
    hn = MLA_HEADS * MLA_NOPE
    n_pair = MLA_HEADS * MLA_ROPE // LANES
    groups = [slice(c * ATTN_TK, (c + 1) * ATTN_TK) for c in range(vt_ref.shape[0])]
    a = [_rms(h_ref[r, :], g_ref[...]).astype(BF16) for r in groups]
    proj = [_dot(ak, win_ref[...]) for ak in a]
    c_q = [_rms(p[:, :nq], qg_ref[...]).astype(BF16) for p in proj]
    c_kv = [_rms(p[:, nq:nq + nkv], kvg_ref[...]).astype(BF16) for p in proj]
    q = [_dot(x, wuq_ref[...]) for x in c_q]
    kv = [_dot(x, wukv_ref[...]) for x in c_kv]
    ones = jnp.ones((VT_ROWS - MLA_V, ATTN_TK), BF16)
    lane = lax.broadcasted_iota(jnp.int32, (ATTN_TK, LANES), 1)
    first_half = lane % MLA_ROPE < MLA_ROPE // 2

    def rope(x, cos, sin):
        swapped = jnp.where(first_half, pltpu.roll(x, LANES - MLA_ROPE // 2, 1), pltpu.roll(x, MLA_ROPE // 2, 1))
        return x * cos + swapped * sin

    for c, r in enumerate(groups):
        cos = jnp.concatenate([cos_ref[r, :]] * 4, axis=1)
        sn = sin_ref[r, :]
        sin = jnp.concatenate([-sn, sn, -sn, sn], axis=1)
        qn_ref[r, :] = (q[c][:, :hn] * q_scale).astype(BF16)
        for p in range(n_pair):
            qp = q[c][:, hn + p * LANES:hn + (p + 1) * LANES]
            qr_ref[r, p * LANES:(p + 1) * LANES] = (rope(qp, cos, sin) * q_scale).astype(BF16)
        k_rope = rope(proj[c][:, nq + nkv:], cos, sin)
        kr_even = jnp.where(lane < MLA_ROPE, k_rope, 0.0).astype(BF16)
        kr_odd = jnp.where(lane >= MLA_ROPE, k_rope, 0.0).astype(BF16)
        vt = kv[c][:, hn:].T.astype(BF16)
        for hd in range(MLA_HEADS):
            vt_ref[c, hd * VT_ROWS:hd * VT_ROWS + MLA_V, :] = vt[hd * MLA_V:(hd + 1) * MLA_V, :]
            vt_ref[c, hd * VT_ROWS + MLA_V:(hd + 1) * VT_ROWS, :] = ones
            off = (hd % 2) * KEY_WIDTH
            kf_ref[0, hd // 2, r, off:off + MLA_NOPE] = kv[c][:, hd * MLA_NOPE:(hd + 1) * MLA_NOPE].astype(BF16)
            kf_ref[0, hd // 2, r, off + MLA_NOPE:off + KEY_WIDTH] = kr_even if hd % 2 == 0 else kr_odd


def _mla_proj(h, gain, w_in, q_gain, kv_gain, w_uq, w_ukv, slot, cos, sin, q_scale, bsz):
    t = h.shape[0]
    tm = ROW_TILE
    n_s = t // bsz // tm
    row = lambda w: pl.BlockSpec((tm, w), lambda i: (i, 0))
    hn = MLA_HEADS * MLA_NOPE
    return pl.pallas_call(
        functools.partial(_mla_proj_kernel, q_scale=q_scale),
        grid=(t // tm,),
        in_specs=[row(D_MODEL), _const_spec(gain.shape), _layer_spec(w_in, slot), _const_spec(q_gain.shape),
                  _const_spec(kv_gain.shape), _layer_spec(w_uq, slot), _layer_spec(w_ukv, slot),
                  row(MLA_ROPE // 2), row(MLA_ROPE // 2)],
        out_specs=[row(hn), row(MLA_HEADS * MLA_ROPE),
                   pl.BlockSpec((1, MLA_HEADS // 2, tm, KEY_PAIR_WIDTH), lambda i: (i // n_s, 0, i % n_s, 0)),
                   pl.BlockSpec((tm // ATTN_TK, MLA_HEADS * VT_ROWS, ATTN_TK), lambda i: (i, 0, 0))],
        out_shape=[jax.ShapeDtypeStruct((t, hn), BF16), jax.ShapeDtypeStruct((t, MLA_HEADS * MLA_ROPE), BF16),
                   jax.ShapeDtypeStruct((bsz, MLA_HEADS // 2, t // bsz, KEY_PAIR_WIDTH), BF16),
                   jax.ShapeDtypeStruct((t // ATTN_TK, MLA_HEADS * VT_ROWS, ATTN_TK), BF16)],
        compiler_params=_params("parallel"),
        name="mla_proj",
    )(h, gain, w_in, q_gain, kv_gain, w_uq, w_ukv, cos, sin)


def _attn_kernel(qn_ref, qr_ref, k_ref, vt_ref, o_ref, qt_ref, sa_ref, sb_ref, pa_ref, pb_ref, aa_ref, ab_ref,
                 m_ref, l_ref, acc_ref):
    tq, tk = ATTN_TQ, ATTN_TK
    diag = tq // tk
    heads = range(ATTN_HEADS)

    def scores(j, st_ref):
        start = pl.multiple_of(j * tk, tk)
        for hd in heads:
            kc = (hd % 2) * KEY_WIDTH
            st_ref[hd] = _dot(k_ref[0, hd // 2, pl.ds(start, tk), kc:kc + KEY_WIDTH], qt_ref[hd])

    def softmax(st_ref, pt_ref, alpha_ref, mask_from=None):
        for hd in heads:
            st = st_ref[hd]
            if mask_from is not None:
                key = lax.broadcasted_iota(jnp.int32, st.shape, 0) + mask_from
                qry = lax.broadcasted_iota(jnp.int32, st.shape, 1)
                st = jnp.where(qry >= key, st, -jnp.inf)
            m = m_ref[hd]
            m_new = jnp.maximum(m, jnp.max(st, axis=0, keepdims=True))
            alpha = jnp.exp2(m - m_new)
            pt = jnp.exp2(st - m_new)
            m_ref[hd] = m_new
            pt_ref[hd] = pt.astype(BF16)
            alpha_ref[hd] = alpha

    def accumulate(j, pt_ref, alpha_ref):
        for hd in heads:
            pv = _dot(vt_ref[0, j, hd * VT_ROWS:(hd + 1) * VT_ROWS, :], pt_ref[hd])
            acc_ref[hd] = alpha_ref[hd] * acc_ref[hd] + pv[:MLA_V]
            l_ref[hd] = alpha_ref[hd] * l_ref[hd] + pv[MLA_V:MLA_V + 1]

    bufs = ((sa_ref, pa_ref, aa_ref), (sb_ref, pb_ref, ab_ref))

    def run(j0, n, last):
        for idx in range(n):
            (s_cur, p_cur, a_cur), (s_nxt, p_prv, a_prv) = bufs[idx % 2], bufs[1 - idx % 2]
            if not (last and idx == n - 1):
                scores(j0 + idx + 1, s_nxt)
            accumulate(jnp.maximum(j0 + idx - 1, 0), p_prv, a_prv)
            on_diag = last and idx >= n - diag
            softmax(s_cur, p_cur, a_cur, mask_from=(idx - (n - diag)) * tk if on_diag else None)
        if last:
            accumulate(j0 + n - 1, p_cur, a_cur)

    def group(g, carry):
        run(ATTN_UNROLL * g, ATTN_UNROLL, last=False)
        return carry

    n_tiles = qn_ref.shape[1] // tq

    def first_scores(i):
        rows = pl.ds(pl.multiple_of(i * tq, tq), tq)
        for hd in heads:
            qt_ref[hd, :LANES, :] = qn_ref[0, rows, hd * LANES:(hd + 1) * LANES].T
            qt_ref[hd, LANES:, :] = qr_ref[0, rows, (hd // 2) * LANES:(hd // 2 + 1) * LANES].T
        scores(0, sa_ref)

    def tile(i, carry):
        rows = pl.ds(pl.multiple_of(i * tq, tq), tq)
        m_ref[...] = jnp.full(m_ref.shape, -jnp.inf, F32)
        l_ref[...] = jnp.zeros(l_ref.shape, F32)
        acc_ref[...] = jnp.zeros(acc_ref.shape, F32)
        pb_ref[...] = jnp.zeros(pb_ref.shape, BF16)
        ab_ref[...] = jnp.ones(ab_ref.shape, F32)

        below = i * diag
        lax.fori_loop(0, below // ATTN_UNROLL, group, 0)
        for rest in range(0, ATTN_UNROLL, diag):
            @pl.when(below % ATTN_UNROLL == rest)
            def _(rest=rest):
                run(below - rest, rest + diag, last=True)

        first_scores(jnp.minimum(i + 1, n_tiles - 1))
        for hd in heads:
            o_ref[0, rows, hd * MLA_V:(hd + 1) * MLA_V] = (acc_ref[hd] / l_ref[hd]).T.astype(o_ref.dtype)
        return carry

    first_scores(0)
    lax.fori_loop(0, n_tiles, tile, 0)


def _attention(qn, qr, kf, vt):
    b, s, _ = qn.shape
    tq, tk = ATTN_TQ, ATTN_TK
    nh = ATTN_HEADS
    assert nh == 2 and 2 * MLA_ROPE == LANES, "one grid step handles one head pair sharing 128 rope lanes"
    assert s % tq == 0 and tq % tk == 0 and ATTN_UNROLL % 2 == 0 and ATTN_UNROLL % (tq // tk) == 0
    return pl.pallas_call(
        _attn_kernel,
        grid=(b, MLA_HEADS // nh),
        in_specs=[pl.BlockSpec((1, s, nh * LANES), lambda bi, hg: (bi, 0, hg)),
                  pl.BlockSpec((1, s, nh * LANES // 2), lambda bi, hg: (bi, 0, hg)),
                  pl.BlockSpec((1, nh // 2, s, KEY_PAIR_WIDTH), lambda bi, hg: (bi, hg, 0, 0)),
                  pl.BlockSpec((1, s // tk, nh * VT_ROWS, tk), lambda bi, hg: (bi, 0, hg, 0))],
        out_specs=pl.BlockSpec((1, s, nh * MLA_V), lambda bi, hg: (bi, 0, hg)),
        out_shape=jax.ShapeDtypeStruct((b, s, MLA_HEADS * MLA_V), BF16),
        scratch_shapes=[pltpu.VMEM((nh, KEY_WIDTH, tq), BF16), pltpu.VMEM((nh, tk, tq), F32),
                        pltpu.VMEM((nh, tk, tq), F32), pltpu.VMEM((nh, tk, tq), BF16),
                        pltpu.VMEM((nh, tk, tq), BF16), pltpu.VMEM((nh, 1, tq), F32), pltpu.VMEM((nh, 1, tq), F32),
                        pltpu.VMEM((nh, 1, tq), F32), pltpu.VMEM((nh, 1, tq), F32),
                        pltpu.VMEM((nh, MLA_V, tq), F32)],
        compiler_params=_params("parallel", "parallel"),
        name="mla_attention",
    )(qn, qr, kf, vt)


def _hgrn_proj_kernel(h_ref, g_ref, win_ref, lb_ref, q_ref, f_ref, v_ref, gate_ref):
    hk = HGRN_HEADS * HGRN_DK
    tm = h_ref.shape[0]
    lb = lb_ref[...]
    groups = [slice(k * tm // ROW_GROUPS, (k + 1) * tm // ROW_GROUPS) for k in range(ROW_GROUPS)]
    a = [_rms(h_ref[r, :], g_ref[...]).astype(BF16) for r in groups]
    for r, ak in zip(groups, a):
        qx = _dot(ak, win_ref[:, :hk])
        q_ref[r, :] = qx * jax.nn.sigmoid(qx)
    for r, ak in zip(groups, a):
        fx = _dot(ak, win_ref[:, hk:2 * hk])
        f_ref[r, :] = lb + (1.0 - lb) * jax.nn.sigmoid(fx)
    for r, ak in zip(groups, a):
        v_ref[r, :] = _dot(ak, win_ref[:, 2 * hk:3 * hk]).astype(BF16)
    for r, ak in zip(groups, a):
        gx = _dot(ak, win_ref[:, 3 * hk:])
        gate_ref[r, :] = (gx * jax.nn.sigmoid(gx)).astype(BF16)


def _hgrn_proj(h, gain, w_in, slot, lb):
    t = h.shape[0]
    tm = ROW_TILE
    hk = HGRN_HEADS * HGRN_DK
    row = pl.BlockSpec((tm, hk), lambda i: (i, 0))
    return pl.pallas_call(
        _hgrn_proj_kernel,
        grid=(t // tm,),
        in_specs=[row, _const_spec(gain.shape), _layer_spec(w_in, slot), _const_spec(lb.shape)],
        out_specs=[row, row, row, row],
        out_shape=[jax.ShapeDtypeStruct((t, hk), F32), jax.ShapeDtypeStruct((t, hk), F32),
                   jax.ShapeDtypeStruct((t, hk), BF16), jax.ShapeDtypeStruct((t, hk), BF16)],
        compiler_params=_params("parallel"),
        name="hgrn_proj",
    )(h, gain, w_in, lb)


def _chunk_cumsum(x, pos):
    shift = 1
    while shift < HGRN_CHUNK:
        x = x + jnp.where(pos >= shift, pltpu.roll(x, shift, 0), 0.0)
        shift *= 2
    return x


def _hgrn_rec_kernel(q_ref, f_ref, v_ref, gate_ref, on_ref, y_ref, state_ref):
    c = HGRN_CHUNK
    tl = HGRN_TILE
    nsub = tl // c

    @pl.when(pl.program_id(2) == 0)
    def _():
        state_ref[...] = jnp.zeros_like(state_ref)

    heads = range(HGRN_STEP_HEADS)
    chunks = [slice(n * c, (n + 1) * c) for n in range(nsub)]
    pos = lax.broadcasted_iota(jnp.int32, (tl, HGRN_DK), 0) % c
    row = lax.broadcasted_iota(jnp.int32, (tl, tl), 0)
    col = lax.broadcasted_iota(jnp.int32, (tl, tl), 1)
    keep = (row // c == col // c) & (row >= col)

    vs, q_decs, decays, scores, kvts = [], [], [], [], []
    for hd in heads:
        cols = slice(hd * HGRN_DK, (hd + 1) * HGRN_DK)
        q = q_ref[0, :, cols]
        f = f_ref[0, :, cols]
        v = v_ref[0, :, cols]
        k = 1.0 - f
        b = _chunk_cumsum(jnp.log2(f), pos)

        def per_chunk(fn, b=b):
            return jnp.concatenate([jnp.broadcast_to(fn(b[rows, :]), (c, HGRN_DK)) for rows in chunks], axis=0)

        b_ref = per_chunk(lambda bc: bc[c // 2:c // 2 + 1])
        q_rel = q * jnp.exp2(b - b_ref)
        k_rel = k * jnp.exp2(b_ref - b)
        q_dec = q_rel * per_chunk(lambda bc: jnp.exp2(bc[c // 2:c // 2 + 1]))
        k_dec = (k_rel * per_chunk(lambda bc: jnp.exp2(bc[c - 1:c] - bc[c // 2:c // 2 + 1]))).astype(BF16)
        scores.append(_dot_nt(q_rel.astype(BF16), k_rel.astype(BF16)))
        kvts.append([_dot_tn(v[rows], k_dec[rows]) for rows in chunks])
        q_decs.append(q_dec.astype(BF16))
        decays.append([jnp.exp2(b[rows.stop - 1:rows.stop, :]) for rows in chunks])
        vs.append(v)

    for hd in heads:
        st = state_ref[hd]
        states = []
        for n in range(nsub):
            states.append(st.astype(BF16))
            st = st * decays[hd][n] + kvts[hd][n]
        state_ref[hd] = st
        o = _dot(jnp.where(keep, scores[hd], 0.0).astype(BF16), vs[hd])
        o = o + jnp.concatenate([_dot_nt(q_decs[hd][rows], states[n]) for n, rows in enumerate(chunks)], axis=0)
        cols = slice(hd * HGRN_DV, (hd + 1) * HGRN_DV)
        y = _rms(o, on_ref[...]) * gate_ref[0, :, cols].astype(F32)
        y_ref[0, :, cols] = y.astype(y_ref.dtype)


def _hgrn_rec(q, f, v, gate, o_gain):
    b, s, _ = q.shape
    nh = HGRN_STEP_HEADS
    blk = pl.BlockSpec((1, HGRN_TILE, nh * HGRN_DK), lambda bi, hg, i: (bi, i, hg))
    return pl.pallas_call(
        _hgrn_rec_kernel,
        grid=(b, HGRN_HEADS // nh, s // HGRN_TILE),
        in_specs=[blk, blk, blk, blk, _const_spec(o_gain.shape)],
        out_specs=blk,
        out_shape=jax.ShapeDtypeStruct((b, s, HGRN_HEADS * HGRN_DV), BF16),
        scratch_shapes=[pltpu.VMEM((nh, HGRN_DV, HGRN_DK), F32)],
        compiler_params=_params("parallel", "parallel", "arbitrary"),
        name="hgrn_recurrence",
    )(q, f, v, gate, o_gain)


def _post_mlp_kernel(h_ref, y_ref, wo_ref, g_ref, w1_ref, w2_ref, out_ref, u_ref):
    g = g_ref[...]
    tm = h_ref.shape[0]
    groups = [slice(k * tm // ROW_GROUPS, (k + 1) * tm // ROW_GROUPS) for k in range(ROW_GROUPS)]
    mixed = [_dot(y_ref[r, :], wo_ref[...]) for r in groups]
    h1 = [h_ref[r, :] + _rms(m, g[1:2]) for r, m in zip(groups, mixed)]
    a = [_rms(x, g[2:3]).astype(BF16) for x in h1]
    for c in range(D_FF // FF_CHUNK):
        cols = slice(c * FF_CHUNK, (c + 1) * FF_CHUNK)
        for r, ak in zip(groups, a):
            u = jnp.maximum(_dot(ak, w1_ref[:, cols]), 0.0)
            u_ref[r, cols] = (u * u).astype(BF16)
    down = [_dot(u_ref[r, :], w2_ref[...]) for r in groups]
    for r, x, d in zip(groups, h1, down):
        out_ref[r, :] = x + _rms(d, g[3:4])


def _post_mlp(h, y, w_o, slot, gains, w1, w2, layer):
    t = h.shape[0]
    tm = MLP_ROW_TILE
    row = pl.BlockSpec((tm, D_MODEL), lambda i: (i, 0))
    return pl.pallas_call(
        _post_mlp_kernel,
        grid=(t // tm,),
        in_specs=[row, row, _layer_spec(w_o, slot), _layer_spec(gains, layer), _layer_spec(w1, layer),
                  _layer_spec(w2, layer)],
        out_specs=row,
        out_shape=jax.ShapeDtypeStruct((t, D_MODEL), F32),
        scratch_shapes=[pltpu.VMEM((tm, D_FF), BF16)],
        compiler_params=_params("parallel"),
        name="post_mlp",
    )(h, y, w_o, gains, w1, w2)


def _mla_weights(w_in, w_uq, w_ukv):
    n, nq, nkv = w_in.shape[0], MLA_Q_LORA, MLA_KV_LORA
    kr = w_in[:, :, nq + nkv:]
    w_in_x = jnp.concatenate([w_in[:, :, :nq + nkv], kr, kr], axis=2).astype(BF16)
    uq = w_uq.reshape(n, nq, MLA_HEADS, MLA_NOPE + MLA_ROPE)
    w_uq_x = jnp.concatenate([uq[..., :MLA_NOPE].reshape(n, nq, -1), uq[..., MLA_NOPE:].reshape(n, nq, -1)],
                             axis=2).astype(BF16)
    ukv = w_ukv.reshape(n, nkv, MLA_HEADS, MLA_NOPE + MLA_V)
    w_ukv_x = jnp.concatenate([ukv[..., :MLA_NOPE].reshape(n, nkv, -1), ukv[..., MLA_NOPE:].reshape(n, nkv, -1)],
                              axis=2).astype(BF16)
    return w_in_x, w_uq_x, w_ukv_x


def kernel(x, positions, norm_gains, mla_w_in, mla_q_norm, mla_kv_norm, mla_w_uq, mla_w_ukv, mla_w_o, hgrn_w_in,
           hgrn_lb_logits, hgrn_o_norm, hgrn_w_o, mlp_w1, mlp_w2):
    bsz, seq, d = x.shape
    t = bsz * seq
    assert d == D_MODEL and seq % max(ROW_TILE, MLP_ROW_TILE, ATTN_TQ, HGRN_TILE) == 0
    assert ROW_TILE % ATTN_TK == 0 and HGRN_TILE % HGRN_CHUNK == 0 and HGRN_HEADS % HGRN_STEP_HEADS == 0

    inv_freq = jnp.power(ROPE_BASE, -jnp.arange(0, MLA_ROPE, 2, dtype=F32) / MLA_ROPE)
    ang = positions.astype(F32).reshape(t, 1) * inv_freq
    ang = ang.reshape(-1, LANES)
    cos, sin = lax.optimization_barrier((jnp.cos(ang), jnp.sin(ang)))
    cos, sin = cos.reshape(t, -1), sin.reshape(t, -1)
    cos_t, sin_t = cos, sin
    p = jax.nn.softmax(hgrn_lb_logits.astype(F32), axis=0)
    lower_bounds = jnp.cumsum(p, axis=0) - p[0]
    q_scale = (MLA_NOPE + MLA_ROPE) ** -0.5 * math.log2(math.e)

    mla_w_o, hgrn_w_in, hgrn_w_o, mlp_w1, mlp_w2 = (
        w.astype(BF16) for w in (mla_w_o, hgrn_w_in, hgrn_w_o, mlp_w1, mlp_w2))
    mla_w_in, mla_w_uq, mla_w_ukv = _mla_weights(mla_w_in, mla_w_uq, mla_w_ukv)

    h = x.reshape(t, d)
    for layer in range(DEPTH):
        slot = layer // N_MIXERS
        if layer % N_MIXERS == 0:
            qn, qr, kf, vt = _mla_proj(h, norm_gains[layer, 0:1], mla_w_in, mla_q_norm[slot][None],
                                       mla_kv_norm[slot][None], mla_w_uq, mla_w_ukv, slot, cos_t, sin_t,
                                       q_scale, bsz)
            y = _attention(qn.reshape(bsz, seq, -1), qr.reshape(bsz, seq, -1), kf,
                           vt.reshape(bsz, seq // ATTN_TK, -1, ATTN_TK))
            w_o = mla_w_o
        else:
            q, f, v, gate = _hgrn_proj(h, norm_gains[layer, 0:1], hgrn_w_in, slot, lower_bounds[layer][None])
            shp = (bsz, seq, -1)
            y = _hgrn_rec(q.reshape(shp), f.reshape(shp), v.reshape(shp), gate.reshape(shp),
                          hgrn_o_norm[slot][None])
            w_o = hgrn_w_o
        h = _post_mlp(h, y.reshape(t, -1), w_o, slot, norm_gains, mlp_w1, mlp_w2, layer)
    return h.reshape(bsz, seq, d)
```

```python
import functools
import math

import jax
import jax.numpy as jnp
from jax import lax
from jax.experimental import pallas as pl
from jax.experimental.pallas import tpu as pltpu

D_MODEL = 1024
DEPTH = 4
N_MIXERS = 2

MLA_HEADS = 8
MLA_Q_LORA = 512
MLA_KV_LORA = 256
MLA_NOPE = 128
MLA_ROPE = 64
MLA_V = 128
ROPE_BASE = 10000.0

HGRN_HEADS = 8
HGRN_DK = D_MODEL // HGRN_HEADS
HGRN_DV = D_MODEL // HGRN_HEADS
HGRN_CHUNK = 32

D_FF = 4 * D_MODEL
EPS = 1e-6

LANES = 128
BF16_SUBLANES = 16
VMEM_LIMIT = 56 * 1024 * 1024

ROW_TILE = 512
MLP_ROW_TILE = 1024
ROW_GROUPS = 2
FF_CHUNK = 1024
ATTN_TQ = 512
ATTN_TK = 256
ATTN_HEADS = 2
ATTN_UNROLL = 8
VT_ROWS = MLA_V + BF16_SUBLANES
KEY_WIDTH = MLA_NOPE + LANES
KEY_PAIR_WIDTH = 2 * KEY_WIDTH
HGRN_TILE = 256
HGRN_STEP_HEADS = 8

F32 = jnp.float32
BF16 = jnp.bfloat16


def _rms(x, g):
    return x * lax.rsqrt(jnp.mean(x * x, axis=-1, keepdims=True) + EPS) * g


def _dot(a, b):
    return jnp.dot(a, b, preferred_element_type=F32)


def _dot_nt(a, b):
    return lax.dot_general(a, b, (((1,), (1,)), ((), ())), preferred_element_type=F32)


def _dot_tn(a, b):
    return lax.dot_general(a, b, (((0,), (0,)), ((), ())), preferred_element_type=F32)


def _const_spec(shape):
    return pl.BlockSpec(shape, lambda *_: (0,) * len(shape))


def _layer_spec(stack, layer):
    return pl.BlockSpec((None,) + stack.shape[1:], lambda *_: (layer,) + (0,) * (stack.ndim - 1),
                        pipeline_mode=pl.Buffered(1))


def _params(*sem):
    return pltpu.CompilerParams(dimension_semantics=sem, vmem_limit_bytes=VMEM_LIMIT)


def _mla_proj_kernel(h_ref, g_ref, win_ref, qg_ref, kvg_ref, wuq_ref, wukv_ref, cos_ref, sin_ref,
                     qn_ref, qr_ref, kf_ref, vt_ref, *, q_scale):
    nq, nkv = MLA_Q_LORA, MLA_KV_LORA
    hn = MLA_HEADS * MLA_NOPE
    n_pair = MLA_HEADS * MLA_ROPE // LANES
    groups = [slice(c * ATTN_TK, (c + 1) * ATTN_TK) for c in range(vt_ref.shape[0])]
    a = [_rms(h_ref[r, :], g_ref[...]).astype(BF16) for r in groups]
    proj = [_dot(ak, win_ref[...]) for ak in a]
    c_q = [_rms(p[:, :nq], qg_ref[...]).astype(BF16) for p in proj]
    c_kv = [_rms(p[:, nq:nq + nkv], kvg_ref[...]).astype(BF16) for p in proj]
    q = [_dot(x, wuq_ref[...]) for x in c_q]
    kv = [_dot(x, wukv_ref[...]) for x in c_kv]
    ones = jnp.ones((VT_ROWS - MLA_V, ATTN_TK), BF16)
    lane = lax.broadcasted_iota(jnp.int32, (ATTN_TK, LANES), 1)
    first_half = lane % MLA_ROPE < MLA_ROPE // 2

    def rope(x, cos, sin):
        swapped = jnp.where(first_half, pltpu.roll(x, LANES - MLA_ROPE // 2, 1), pltpu.roll(x, MLA_ROPE // 2, 1))
        return x * cos + swapped * sin

    for c, r in enumerate(groups):
        cos = jnp.concatenate([cos_ref[r, :]] * 4, axis=1)
        sn = sin_ref[r, :]
        sin = jnp.concatenate([-sn, sn, -sn, sn], axis=1)
        qn_ref[r, :] = (q[c][:, :hn] * q_scale).astype(BF16)
        for p in range(n_pair):
            qp = q[c][:, hn + p * LANES:hn + (p + 1) * LANES]
            qr_ref[r, p * LANES:(p + 1) * LANES] = (rope(qp, cos, sin) * q_scale).astype(BF16)
        k_rope = rope(proj[c][:, nq + nkv:], cos, sin)
        kr_even = jnp.where(lane < MLA_ROPE, k_rope, 0.0).astype(BF16)
        kr_odd = jnp.where(lane >= MLA_ROPE, k_rope, 0.0).astype(BF16)
        vt = kv[c][:, hn:].T.astype(BF16)
        for hd in range(MLA_HEADS):
            vt_ref[c, hd * VT_ROWS:hd * VT_ROWS + MLA_V, :] = vt[hd * MLA_V:(hd + 1) * MLA_V, :]
            vt_ref[c, hd * VT_ROWS + MLA_V:(hd + 1) * VT_ROWS, :] = ones
            off = (hd % 2) * KEY_WIDTH
            kf_ref[0, hd // 2, r, off:off + MLA_NOPE] = kv[c][:, hd * MLA_NOPE:(hd + 1) * MLA_NOPE].astype(BF16)
            kf_ref[0, hd // 2, r, off + MLA_NOPE:off + KEY_WIDTH] = kr_even if hd % 2 == 0 else kr_odd


def _mla_proj(h, gain, w_in, q_gain, kv_gain, w_uq, w_ukv, slot, cos, sin, q_scale, bsz):
    t = h.shape[0]
    tm = ROW_TILE
    n_s = t // bsz // tm
    row = lambda w: pl.BlockSpec((tm, w), lambda i: (i, 0))
    hn = MLA_HEADS * MLA_NOPE
    return pl.pallas_call(
        functools.partial(_mla_proj_kernel, q_scale=q_scale),
        grid=(t // tm,),
        in_specs=[row(D_MODEL), _const_spec(gain.shape), _layer_spec(w_in, slot), _const_spec(q_gain.shape),
                  _const_spec(kv_gain.shape), _layer_spec(w_uq, slot), _layer_spec(w_ukv, slot),
                  row(MLA_ROPE // 2), row(MLA_ROPE // 2)],
        out_specs=[row(hn), row(MLA_HEADS * MLA_ROPE),
                   pl.BlockSpec((1, MLA_HEADS // 2, tm, KEY_PAIR_WIDTH), lambda i: (i // n_s, 0, i % n_s, 0)),
                   pl.BlockSpec((tm // ATTN_TK, MLA_HEADS * VT_ROWS, ATTN_TK), lambda i: (i, 0, 0))],
        out_shape=[jax.ShapeDtypeStruct((t, hn), BF16), jax.ShapeDtypeStruct((t, MLA_HEADS * MLA_ROPE), BF16),
                   jax.ShapeDtypeStruct((bsz, MLA_HEADS // 2, t // bsz, KEY_PAIR_WIDTH), BF16),
                   jax.ShapeDtypeStruct((t // ATTN_TK, MLA_HEADS * VT_ROWS, ATTN_TK), BF16)],
        compiler_params=_params("parallel"),
        name="mla_proj",
    )(h, gain, w_in, q_gain, kv_gain, w_uq, w_ukv, cos, sin)


def _attn_kernel(qn_ref, qr_ref, k_ref, vt_ref, o_ref, qt_ref, sa_ref, sb_ref, pa_ref, pb_ref, aa_ref, ab_ref,
                 m_ref, l_ref, acc_ref):
    tq, tk = ATTN_TQ, ATTN_TK
    diag = tq // tk
    heads = range(ATTN_HEADS)

    def scores(j, st_ref):
        start = pl.multiple_of(j * tk, tk)
        for hd in heads:
            kc = (hd % 2) * KEY_WIDTH
            st_ref[hd] = _dot(k_ref[0, hd // 2, pl.ds(start, tk), kc:kc + KEY_WIDTH], qt_ref[hd])

    def softmax(st_ref, pt_ref, alpha_ref, mask_from=None):
        for hd in heads:
            st = st_ref[hd]
            if mask_from is not None:
                key = lax.broadcasted_iota(jnp.int32, st.shape, 0) + mask_from
                qry = lax.broadcasted_iota(jnp.int32, st.shape, 1)
                st = jnp.where(qry >= key, st, -jnp.inf)
            m = m_ref[hd]
            m_new = jnp.maximum(m, jnp.max(st, axis=0, keepdims=True))
            alpha = jnp.exp2(m - m_new)
            pt = jnp.exp2(st - m_new)
            m_ref[hd] = m_new
            pt_ref[hd] = pt.astype(BF16)
            alpha_ref[hd] = alpha

    def accumulate(j, pt_ref, alpha_ref):
        for hd in heads:
            pv = _dot(vt_ref[0, j, hd * VT_ROWS:(hd + 1) * VT_ROWS, :], pt_ref[hd])
            acc_ref[hd] = alpha_ref[hd] * acc_ref[hd] + pv[:MLA_V]
            l_ref[hd] = alpha_ref[hd] * l_ref[hd] + pv[MLA_V:MLA_V + 1]

    bufs = ((sa_ref, pa_ref, aa_ref), (sb_ref, pb_ref, ab_ref))

    def run(j0, n, last):
        for idx in range(n):
            (s_cur, p_cur, a_cur), (s_nxt, p_prv, a_prv) = bufs[idx % 2], bufs[1 - idx % 2]
            if not (last and idx == n - 1):
                scores(j0 + idx + 1, s_nxt)
            accumulate(jnp.maximum(j0 + idx - 1, 0), p_prv, a_prv)
            on_diag = last and idx >= n - diag
            softmax(s_cur, p_cur, a_cur, mask_from=(idx - (n - diag)) * tk if on_diag else None)
        if last:
            accumulate(j0 + n - 1, p_cur, a_cur)

    def group(g, carry):
        run(ATTN_UNROLL * g, ATTN_UNROLL, last=False)
        return carry

    n_tiles = qn_ref.shape[1] // tq

    def first_scores(i):
        rows = pl.ds(pl.multiple_of(i * tq, tq), tq)
        for hd in heads:
            qt_ref[hd, :LANES, :] = qn_ref[0, rows, hd * LANES:(hd + 1) * LANES].T
            qt_ref[hd, LANES:, :] = qr_ref[0, rows, (hd // 2) * LANES:(hd // 2 + 1) * LANES].T
        scores(0, sa_ref)

    def tile(i, carry):
        rows = pl.ds(pl.multiple_of(i * tq, tq), tq)
        m_ref[...] = jnp.full(m_ref.shape, -jnp.inf, F32)
        l_ref[...] = jnp.zeros(l_ref.shape, F32)
        acc_ref[...] = jnp.zeros(acc_ref.shape, F32)
        pb_ref[...] = jnp.zeros(pb_ref.shape, BF16)
        ab_ref[...] = jnp.ones(ab_ref.shape, F32)

        below = i * diag
        lax.fori_loop(0, below // ATTN_UNROLL, group, 0)
        for rest in range(0, ATTN_UNROLL, diag):
            @pl.when(below % ATTN_UNROLL == rest)
            def _(rest=rest):
                run(below - rest, rest + diag, last=True)

        first_scores(jnp.minimum(i + 1, n_tiles - 1))
        for hd in heads:
            o_ref[0, rows, hd * MLA_V:(hd + 1) * MLA_V] = (acc_ref[hd] / l_ref[hd]).T.astype(o_ref.dtype)
        return carry

    first_scores(0)
    lax.fori_loop(0, n_tiles, tile, 0)


def _attention(qn, qr, kf, vt):
    b, s, _ = qn.shape
    tq, tk = ATTN_TQ, ATTN_TK
    nh = ATTN_HEADS
    assert nh == 2 and 2 * MLA_ROPE == LANES, "one grid step handles one head pair sharing 128 rope lanes"
    assert s % tq == 0 and tq % tk == 0 and ATTN_UNROLL % 2 == 0 and ATTN_UNROLL % (tq // tk) == 0
    return pl.pallas_call(
        _attn_kernel,
        grid=(b, MLA_HEADS // nh),
        in_specs=[pl.BlockSpec((1, s, nh * LANES), lambda bi, hg: (bi, 0, hg)),
                  pl.BlockSpec((1, s, nh * LANES // 2), lambda bi, hg: (bi, 0, hg)),
                  pl.BlockSpec((1, nh // 2, s, KEY_PAIR_WIDTH), lambda bi, hg: (bi, hg, 0, 0)),
                  pl.BlockSpec((1, s // tk, nh * VT_ROWS, tk), lambda bi, hg: (bi, 0, hg, 0))],
        out_specs=pl.BlockSpec((1, s, nh * MLA_V), lambda bi, hg: (bi, 0, hg)),
        out_shape=jax.ShapeDtypeStruct((b, s, MLA_HEADS * MLA_V), BF16),
        scratch_shapes=[pltpu.VMEM((nh, KEY_WIDTH, tq), BF16), pltpu.VMEM((nh, tk, tq), F32),
                        pltpu.VMEM((nh, tk, tq), F32), pltpu.VMEM((nh, tk, tq), BF16),
                        pltpu.VMEM((nh, tk, tq), BF16), pltpu.VMEM((nh, 1, tq), F32), pltpu.VMEM((nh, 1, tq), F32),
                        pltpu.VMEM((nh, 1, tq), F32), pltpu.VMEM((nh, 1, tq), F32),
                        pltpu.VMEM((nh, MLA_V, tq), F32)],
        compiler_params=_params("parallel", "parallel"),
        name="mla_attention",
    )(qn, qr, kf, vt)


def _hgrn_proj_kernel(h_ref, g_ref, win_ref, lb_ref, q_ref, f_ref, v_ref, gate_ref):
    hk = HGRN_HEADS * HGRN_DK
    tm = h_ref.shape[0]
    lb = lb_ref[...]
    groups = [slice(k * tm // ROW_GROUPS, (k + 1) * tm // ROW_GROUPS) for k in range(ROW_GROUPS)]
    a = [_rms(h_ref[r, :], g_ref[...]).astype(BF16) for r in groups]
    for r, ak in zip(groups, a):
        qx = _dot(ak, win_ref[:, :hk])
        q_ref[r, :] = qx * jax.nn.sigmoid(qx)
    for r, ak in zip(groups, a):
        fx = _dot(ak, win_ref[:, hk:2 * hk])
        f_ref[r, :] = lb + (1.0 - lb) * jax.nn.sigmoid(fx)
    for r, ak in zip(groups, a):
        v_ref[r, :] = _dot(ak, win_ref[:, 2 * hk:3 * hk]).astype(BF16)
    for r, ak in zip(groups, a):
        gx = _dot(ak, win_ref[:, 3 * hk:])
        gate_ref[r, :] = (gx * jax.nn.sigmoid(gx)).astype(BF16)


def _hgrn_proj(h, gain, w_in, slot, lb):
    t = h.shape[0]
    tm = ROW_TILE
    hk = HGRN_HEADS * HGRN_DK
    row = pl.BlockSpec((tm, hk), lambda i: (i, 0))
    return pl.pallas_call(
        _hgrn_proj_kernel,
        grid=(t // tm,),
        in_specs=[row, _const_spec(gain.shape), _layer_spec(w_in, slot), _const_spec(lb.shape)],
        out_specs=[row, row, row, row],
        out_shape=[jax.ShapeDtypeStruct((t, hk), F32), jax.ShapeDtypeStruct((t, hk), F32),
                   jax.ShapeDtypeStruct((t, hk), BF16), jax.ShapeDtypeStruct((t, hk), BF16)],
        compiler_params=_params("parallel"),
        name="hgrn_proj",
    )(h, gain, w_in, lb)


def _chunk_cumsum(x, pos):
    shift = 1
    while shift < HGRN_CHUNK:
        x = x + jnp.where(pos >= shift, pltpu.roll(x, shift, 0), 0.0)
        shift *= 2
    return x


def _hgrn_rec_kernel(q_ref, f_ref, v_ref, gate_ref, on_ref, y_ref, state_ref):
    c = HGRN_CHUNK
    tl = HGRN_TILE
    nsub = tl // c

    @pl.when(pl.program_id(2) == 0)
    def _():
        state_ref[...] = jnp.zeros_like(state_ref)

    heads = range(HGRN_STEP_HEADS)
    chunks = [slice(n * c, (n + 1) * c) for n in range(nsub)]
    pos = lax.broadcasted_iota(jnp.int32, (tl, HGRN_DK), 0) % c
    row = lax.broadcasted_iota(jnp.int32, (tl, tl), 0)
    col = lax.broadcasted_iota(jnp.int32, (tl, tl), 1)
    keep = (row // c == col // c) & (row >= col)

    vs, q_decs, decays, scores, kvts = [], [], [], [], []
    for hd in heads:
        cols = slice(hd * HGRN_DK, (hd + 1) * HGRN_DK)
        q = q_ref[0, :, cols]
        f = f_ref[0, :, cols]
        v = v_ref[0, :, cols]
        k = 1.0 - f
        b = _chunk_cumsum(jnp.log2(f), pos)

        def per_chunk(fn, b=b):
            return jnp.concatenate([jnp.broadcast_to(fn(b[rows, :]), (c, HGRN_DK)) for rows in chunks], axis=0)

        b_ref = per_chunk(lambda bc: bc[c // 2:c // 2 + 1])
        q_rel = q * jnp.exp2(b - b_ref)
        k_rel = k * jnp.exp2(b_ref - b)
        q_dec = q_rel * per_chunk(lambda bc: jnp.exp2(bc[c // 2:c // 2 + 1]))
        k_dec = (k_rel * per_chunk(lambda bc: jnp.exp2(bc[c - 1:c] - bc[c // 2:c // 2 + 1]))).astype(BF16)
        scores.append(_dot_nt(q_rel.astype(BF16), k_rel.astype(BF16)))
        kvts.append([_dot_tn(v[rows], k_dec[rows]) for rows in chunks])
        q_decs.append(q_dec.astype(BF16))
        decays.append([jnp.exp2(b[rows.stop - 1:rows.stop, :]) for rows in chunks])
        vs.append(v)

    for hd in heads:
        st = state_ref[hd]
        states = []
        for n in range(nsub):
            states.append(st.T.astype(BF16))
            st = st * decays[hd][n] + kvts[hd][n]
        state_ref[hd] = st
        o = _dot(jnp.where(keep, scores[hd], 0.0).astype(BF16), vs[hd])
        o = o + jnp.concatenate([_dot(q_decs[hd][rows], states[n]) for n, rows in enumerate(chunks)], axis=0)
        cols = slice(hd * HGRN_DV, (hd + 1) * HGRN_DV)
        y = _rms(o, on_ref[...]) * gate_ref[0, :, cols].astype(F32)
        y_ref[0, :, cols] = y.astype(y_ref.dtype)


def _hgrn_rec(q, f, v, gate, o_gain):
    b, s, _ = q.shape
    nh = HGRN_STEP_HEADS
    blk = pl.BlockSpec((1, HGRN_TILE, nh * HGRN_DK), lambda bi, hg, i: (bi, i, hg))
    return pl.pallas_call(
        _hgrn_rec_kernel,
        grid=(b, HGRN_HEADS // nh, s // HGRN_TILE),
        in_specs=[blk, blk, blk, blk, _const_spec(o_gain.shape)],
        out_specs=blk,
        out_shape=jax.ShapeDtypeStruct((b, s, HGRN_HEADS * HGRN_DV), BF16),
        scratch_shapes=[pltpu.VMEM((nh, HGRN_DV, HGRN_DK), F32)],
        compiler_params=_params("parallel", "parallel", "arbitrary"),
        name="hgrn_recurrence",
    )(q, f, v, gate, o_gain)


def _post_mlp_kernel(h_ref, y_ref, wo_ref, g_ref, w1_ref, w2_ref, out_ref, u_ref):
    g = g_ref[...]
    tm = h_ref.shape[0]
    groups = [slice(k * tm // ROW_GROUPS, (k + 1) * tm // ROW_GROUPS) for k in range(ROW_GROUPS)]
    mixed = [_dot(y_ref[r, :], wo_ref[...]) for r in groups]
    h1 = [h_ref[r, :] + _rms(m, g[1:2]) for r, m in zip(groups, mixed)]
    a = [_rms(x, g[2:3]).astype(BF16) for x in h1]
    for c in range(D_FF // FF_CHUNK):
        cols = slice(c * FF_CHUNK, (c + 1) * FF_CHUNK)
        for r, ak in zip(groups, a):
            u = jnp.maximum(_dot(ak, w1_ref[:, cols]), 0.0)
            u_ref[r, cols] = (u * u).astype(BF16)
    down = [_dot(u_ref[r, :], w2_ref[...]) for r in groups]
    for r, x, d in zip(groups, h1, down):
        out_ref[r, :] = x + _rms(d, g[3:4])


def _post_mlp(h, y, w_o, slot, gains, w1, w2, layer):
    t = h.shape[0]
    tm = MLP_ROW_TILE
    row = pl.BlockSpec((tm, D_MODEL), lambda i: (i, 0))
    return pl.pallas_call(
        _post_mlp_kernel,
        grid=(t // tm,),
        in_specs=[row, row, _layer_spec(w_o, slot), _layer_spec(gains, layer), _layer_spec(w1, layer),
                  _layer_spec(w2, layer)],
        out_specs=row,
        out_shape=jax.ShapeDtypeStruct((t, D_MODEL), F32),
        scratch_shapes=[pltpu.VMEM((tm, D_FF), BF16)],
        compiler_params=_params("parallel"),
        name="post_mlp",
    )(h, y, w_o, gains, w1, w2)


def _mla_weights(w_in, w_uq, w_ukv):
    n, nq, nkv = w_in.shape[0], MLA_Q_LORA, MLA_KV_LORA
    kr = w_in[:, :, nq + nkv:]
    w_in_x = jnp.concatenate([w_in[:, :, :nq + nkv], kr, kr], axis=2).astype(BF16)
    uq = w_uq.reshape(n, nq, MLA_HEADS, MLA_NOPE + MLA_ROPE)
    w_uq_x = jnp.concatenate([uq[..., :MLA_NOPE].reshape(n, nq, -1), uq[..., MLA_NOPE:].reshape(n, nq, -1)],
                             axis=2).astype(BF16)
    ukv = w_ukv.reshape(n, nkv, MLA_HEADS, MLA_NOPE + MLA_V)
    w_ukv_x = jnp.concatenate([ukv[..., :MLA_NOPE].reshape(n, nkv, -1), ukv[..., MLA_NOPE:].reshape(n, nkv, -1)],
                              axis=2).astype(BF16)
    return w_in_x, w_uq_x, w_ukv_x


def kernel(x, positions, norm_gains, mla_w_in, mla_q_norm, mla_kv_norm, mla_w_uq, mla_w_ukv, mla_w_o, hgrn_w_in,
           hgrn_lb_logits, hgrn_o_norm, hgrn_w_o, mlp_w1, mlp_w2):
    bsz, seq, d = x.shape
    t = bsz * seq
    assert d == D_MODEL and seq % max(ROW_TILE, MLP_ROW_TILE, ATTN_TQ, HGRN_TILE) == 0
    assert ROW_TILE % ATTN_TK == 0 and HGRN_TILE % HGRN_CHUNK == 0 and HGRN_HEADS % HGRN_STEP_HEADS == 0

    inv_freq = jnp.power(ROPE_BASE, -jnp.arange(0, MLA_ROPE, 2, dtype=F32) / MLA_ROPE)
    ang = positions.astype(F32).reshape(t, 1) * inv_freq
    ang = ang.reshape(-1, LANES)
    cos, sin = lax.optimization_barrier((jnp.cos(ang), jnp.sin(ang)))
    cos, sin = cos.reshape(t, -1), sin.reshape(t, -1)
    cos_t, sin_t = cos, sin
    p = jax.nn.softmax(hgrn_lb_logits.astype(F32), axis=0)
    lower_bounds = jnp.cumsum(p, axis=0) - p[0]
    q_scale = (MLA_NOPE + MLA_ROPE) ** -0.5 * math.log2(math.e)

    mla_w_o, hgrn_w_in, hgrn_w_o, mlp_w1, mlp_w2 = (
        w.astype(BF16) for w in (mla_w_o, hgrn_w_in, hgrn_w_o, mlp_w1, mlp_w2))
    mla_w_in, mla_w_uq, mla_w_ukv = _mla_weights(mla_w_in, mla_w_uq, mla_w_ukv)

    h = x.reshape(t, d)
    for layer in range(DEPTH):
        slot = layer // N_MIXERS
        if layer % N_MIXERS == 0:
            qn, qr, kf, vt = _mla_proj(h, norm_gains[layer, 0:1], mla_w_in, mla_q_norm[slot][None],
                                       mla_kv_norm[slot][None], mla_w_uq, mla_w_ukv, slot, cos_t, sin_t,
                                       q_scale, bsz)
            y = _attention(qn.reshape(bsz, seq, -1), qr.reshape(bsz, seq, -1), kf,
                           vt.reshape(bsz, seq // ATTN_TK, -1, ATTN_TK))
            w_o = mla_w_o
        else:
            q, f, v, gate = _hgrn_proj(h, norm_gains[layer, 0:1], hgrn_w_in, slot, lower_bounds[layer][None])
            shp = (bsz, seq, -1)
            y = _hgrn_rec(q.reshape(shp), f.reshape(shp), v.reshape(shp), gate.reshape(shp),
                          hgrn_o_norm[slot][None])
            w_o = hgrn_w_o
        h = _post_mlp(h, y.reshape(t, -1), w_o, slot, norm_gains, mlp_w1, mlp_w2, layer)
    return h.reshape(bsz, seq, d)
```

```python
import functools
import math

import jax
import jax.numpy as jnp
from jax import lax
from jax.experimental import pallas as pl
from jax.experimental.pallas import tpu as pltpu

D_MODEL = 1024
DEPTH = 4
N_MIXERS = 2

MLA_HEADS = 8
MLA_Q_LORA = 512
MLA_KV_LORA = 256
MLA_NOPE = 128
MLA_ROPE = 64
MLA_V = 128
ROPE_BASE = 10000.0

HGRN_HEADS = 8
HGRN_DK = D_MODEL // HGRN_HEADS
HGRN_DV = D_MODEL // HGRN_HEADS
HGRN_CHUNK = 32

D_FF = 4 * D_MODEL
EPS = 1e-6

LANES = 128
BF16_SUBLANES = 16
VMEM_LIMIT = 56 * 1024 * 1024

ROW_TILE = 512
MLP_ROW_TILE = 1024
ROW_GROUPS = 2
FF_CHUNK = 1024
ATTN_TQ = 512
ATTN_TK = 256
ATTN_HEADS = 2
ATTN_UNROLL = 8
VT_ROWS = MLA_V + BF16_SUBLANES
KEY_WIDTH = MLA_NOPE + LANES
KEY_PAIR_WIDTH = 2 * KEY_WIDTH
HGRN_TILE = 256
HGRN_STEP_HEADS = 8

F32 = jnp.float32
BF16 = jnp.bfloat16


def _rms(x, g):
    return x * lax.rsqrt(jnp.mean(x * x, axis=-1, keepdims=True) + EPS) * g


def _dot(a, b):
    return jnp.dot(a, b, preferred_element_type=F32)


def _dot_nt(a, b):
    return lax.dot_general(a, b, (((1,), (1,)), ((), ())), preferred_element_type=F32)


def _dot_tn(a, b):
    return lax.dot_general(a, b, (((0,), (0,)), ((), ())), preferred_element_type=F32)


def _const_spec(shape):
    return pl.BlockSpec(shape, lambda *_: (0,) * len(shape))


def _layer_spec(stack, layer):
    return pl.BlockSpec((None,) + stack.shape[1:], lambda *_: (layer,) + (0,) * (stack.ndim - 1),
                        pipeline_mode=pl.Buffered(1))


def _params(*sem):
    return pltpu.CompilerParams(dimension_semantics=sem, vmem_limit_bytes=VMEM_LIMIT)


def _mla_proj_kernel(h_ref, g_ref, win_ref, qg_ref, kvg_ref, wuq_ref, wukv_ref, cos_ref, sin_ref,
                     qn_ref, qr_ref, kf_ref, vt_ref, *, q_scale):
    nq, nkv = MLA_Q_LORA, MLA_KV_LORA
    hn = MLA_HEADS * MLA_NOPE
    n_pair = MLA_HEADS * MLA_ROPE // LANES
    groups = [slice(c * ATTN_TK, (c + 1) * ATTN_TK) for c in range(vt_ref.shape[0])]
    a = [_rms(h_ref[r, :], g_ref[...]).astype(BF16) for r in groups]
    proj = [_dot(ak, win_ref[...]) for ak in a]
    c_q = [_rms(p[:, :nq], qg_ref[...]).astype(BF16) for p in proj]
    c_kv = [_rms(p[:, nq:nq + nkv], kvg_ref[...]).astype(BF16) for p in proj]
    q = [_dot(x, wuq_ref[...]) for x in c_q]
    kv = [_dot(x, wukv_ref[...]) for x in c_kv]
    ones = jnp.ones((VT_ROWS - MLA_V, ATTN_TK), BF16)
    lane = lax.broadcasted_iota(jnp.int32, (ATTN_TK, LANES), 1)
    first_half = lane % MLA_ROPE < MLA_ROPE // 2

    def rope(x, cos, sin):
        swapped = jnp.where(first_half, pltpu.roll(x, LANES - MLA_ROPE // 2, 1), pltpu.roll(x, MLA_ROPE // 2, 1))
        return x * cos + swapped * sin

    for c, r in enumerate(groups):
        cos = jnp.concatenate([cos_ref[r, :]] * 4, axis=1)
        sn = sin_ref[r, :]
        sin = jnp.concatenate([-sn, sn, -sn, sn], axis=1)
        qn_ref[r, :] = (q[c][:, :hn] * q_scale).astype(BF16)
        for p in range(n_pair):
            qp = q[c][:, hn + p * LANES:hn + (p + 1) * LANES]
            qr_ref[r, p * LANES:(p + 1) * LANES] = (rope(qp, cos, sin) * q_scale).astype(BF16)
        k_rope = rope(proj[c][:, nq + nkv:], cos, sin)
        kr_even = jnp.where(lane < MLA_ROPE, k_rope, 0.0).astype(BF16)
        kr_odd = jnp.where(lane >= MLA_ROPE, k_rope, 0.0).astype(BF16)
        vt = kv[c][:, hn:].T.astype(BF16)
        for hd in range(MLA_HEADS):
            vt_ref[c, hd * VT_ROWS:hd * VT_ROWS + MLA_V, :] = vt[hd * MLA_V:(hd + 1) * MLA_V, :]
            vt_ref[c, hd * VT_ROWS + MLA_V:(hd + 1) * VT_ROWS, :] = ones
            off = (hd % 2) * KEY_WIDTH
            kf_ref[0, hd // 2, r, off:off + MLA_NOPE] = kv[c][:, hd * MLA_NOPE:(hd + 1) * MLA_NOPE].astype(BF16)
            kf_ref[0, hd // 2, r, off + MLA_NOPE:off + KEY_WIDTH] = kr_even if hd % 2 == 0 else kr_odd


def _mla_proj(h, gain, w_in, q_gain, kv_gain, w_uq, w_ukv, slot, cos, sin, q_scale, bsz):
    t = h.shape[0]
    tm = ROW_TILE
    n_s = t // bsz // tm
    row = lambda w: pl.BlockSpec((tm, w), lambda i: (i, 0))
    hn = MLA_HEADS * MLA_NOPE
    return pl.pallas_call(
        functools.partial(_mla_proj_kernel, q_scale=q_scale),
        grid=(t // tm,),
        in_specs=[row(D_MODEL), _const_spec(gain.shape), _layer_spec(w_in, slot), _const_spec(q_gain.shape),
                  _const_spec(kv_gain.shape), _layer_spec(w_uq, slot), _layer_spec(w_ukv, slot),
                  row(MLA_ROPE // 2), row(MLA_ROPE // 2)],
        out_specs=[row(hn), row(MLA_HEADS * MLA_ROPE),
                   pl.BlockSpec((1, MLA_HEADS // 2, tm, KEY_PAIR_WIDTH), lambda i: (i // n_s, 0, i % n_s, 0)),
                   pl.BlockSpec((tm // ATTN_TK, MLA_HEADS * VT_ROWS, ATTN_TK), lambda i: (i, 0, 0))],
        out_shape=[jax.ShapeDtypeStruct((t, hn), BF16), jax.ShapeDtypeStruct((t, MLA_HEADS * MLA_ROPE), BF16),
                   jax.ShapeDtypeStruct((bsz, MLA_HEADS // 2, t // bsz, KEY_PAIR_WIDTH), BF16),
                   jax.ShapeDtypeStruct((t // ATTN_TK, MLA_HEADS * VT_ROWS, ATTN_TK), BF16)],
        compiler_params=_params("parallel"),
        name="mla_proj",
    )(h, gain, w_in, q_gain, kv_gain, w_uq, w_ukv, cos, sin)


def _attn_kernel(qn_ref, qr_ref, k_ref, vt_ref, o_ref, qt_ref, sa_ref, sb_ref, pa_ref, pb_ref, aa_ref, ab_ref,
                 m_ref, l_ref, acc_ref):
    tq, tk = ATTN_TQ, ATTN_TK
    diag = tq // tk
    heads = range(ATTN_HEADS)

    def scores(j, st_ref):
        start = pl.multiple_of(j * tk, tk)
        for hd in heads:
            kc = (hd % 2) * KEY_WIDTH
            st_ref[hd] = _dot(k_ref[0, hd // 2, pl.ds(start, tk), kc:kc + KEY_WIDTH], qt_ref[hd])

    def softmax(st_ref, pt_ref, alpha_ref, mask_from=None):
        for hd in heads:
            st = st_ref[hd]
            if mask_from is not None:
                key = lax.broadcasted_iota(jnp.int32, st.shape, 0) + mask_from
                qry = lax.broadcasted_iota(jnp.int32, st.shape, 1)
                st = jnp.where(qry >= key, st, -jnp.inf)
            m = m_ref[hd]
            m_new = jnp.maximum(m, jnp.max(st, axis=0, keepdims=True))
            alpha = jnp.exp2(m - m_new)
            pt = jnp.exp2(st - m_new)
            m_ref[hd] = m_new
            pt_ref[hd] = pt.astype(BF16)
            alpha_ref[hd] = alpha

    def accumulate(j, pt_ref, alpha_ref):
        for hd in heads:
            pv = _dot(vt_ref[0, j, hd * VT_ROWS:(hd + 1) * VT_ROWS, :], pt_ref[hd])
            acc_ref[hd] = alpha_ref[hd] * acc_ref[hd] + pv[:MLA_V]
            l_ref[hd] = alpha_ref[hd] * l_ref[hd] + pv[MLA_V:MLA_V + 1]

    bufs = ((sa_ref, pa_ref, aa_ref), (sb_ref, pb_ref, ab_ref))

    def run(j0, n, last):
        for idx in range(n):
            (s_cur, p_cur, a_cur), (s_nxt, p_prv, a_prv) = bufs[idx % 2], bufs[1 - idx % 2]
            if not (last and idx == n - 1):
                scores(j0 + idx + 1, s_nxt)
            accumulate(jnp.maximum(j0 + idx - 1, 0), p_prv, a_prv)
            on_diag = last and idx >= n - diag
            softmax(s_cur, p_cur, a_cur, mask_from=(idx - (n - diag)) * tk if on_diag else None)
        if last:
            accumulate(j0 + n - 1, p_cur, a_cur)

    def group(g, carry):
        run(ATTN_UNROLL * g, ATTN_UNROLL, last=False)
        return carry

    n_tiles = qn_ref.shape[1] // tq

    def first_scores(i):
        rows = pl.ds(pl.multiple_of(i * tq, tq), tq)
        for hd in heads:
            qt_ref[hd, :LANES, :] = qn_ref[0, rows, hd * LANES:(hd + 1) * LANES].T
            qt_ref[hd, LANES:, :] = qr_ref[0, rows, (hd // 2) * LANES:(hd // 2 + 1) * LANES].T
        scores(0, sa_ref)

    def tile(i, carry):
        rows = pl.ds(pl.multiple_of(i * tq, tq), tq)
        m_ref[...] = jnp.full(m_ref.shape, -jnp.inf, F32)
        l_ref[...] = jnp.zeros(l_ref.shape, F32)
        acc_ref[...] = jnp.zeros(acc_ref.shape, F32)
        pb_ref[...] = jnp.zeros(pb_ref.shape, BF16)
        ab_ref[...] = jnp.ones(ab_ref.shape, F32)

        below = i * diag
        lax.fori_loop(0, below // ATTN_UNROLL, group, 0)
        for rest in range(0, ATTN_UNROLL, diag):
            @pl.when(below % ATTN_UNROLL == rest)
            def _(rest=rest):
                run(below - rest, rest + diag, last=True)

        first_scores(jnp.minimum(i + 1, n_tiles - 1))
        for hd in heads:
            o_ref[0, rows, hd * MLA_V:(hd + 1) * MLA_V] = (acc_ref[hd] / l_ref[hd]).T.astype(o_ref.dtype)
        return carry

    first_scores(0)
    lax.fori_loop(0, n_tiles, tile, 0)


def _attention(qn, qr, kf, vt):
    b, s, _ = qn.shape
    tq, tk = ATTN_TQ, ATTN_TK
    nh = ATTN_HEADS
    assert nh == 2 and 2 * MLA_ROPE == LANES, "one grid step handles one head pair sharing 128 rope lanes"
    assert s % tq == 0 and tq % tk == 0 and ATTN_UNROLL % 2 == 0 and ATTN_UNROLL % (tq // tk) == 0
    return pl.pallas_call(
        _attn_kernel,
        grid=(b, MLA_HEADS // nh),
        in_specs=[pl.BlockSpec((1, s, nh * LANES), lambda bi, hg: (bi, 0, hg)),
                  pl.BlockSpec((1, s, nh * LANES // 2), lambda bi, hg: (bi, 0, hg)),
                  pl.BlockSpec((1, nh // 2, s, KEY_PAIR_WIDTH), lambda bi, hg: (bi, hg, 0, 0)),
                  pl.BlockSpec((1, s // tk, nh * VT_ROWS, tk), lambda bi, hg: (bi, 0, hg, 0))],
        out_specs=pl.BlockSpec((1, s, nh * MLA_V), lambda bi, hg: (bi, 0, hg)),
        out_shape=jax.ShapeDtypeStruct((b, s, MLA_HEADS * MLA_V), BF16),
        scratch_shapes=[pltpu.VMEM((nh, KEY_WIDTH, tq), BF16), pltpu.VMEM((nh, tk, tq), F32),
                        pltpu.VMEM((nh, tk, tq), F32), pltpu.VMEM((nh, tk, tq), BF16),
                        pltpu.VMEM((nh, tk, tq), BF16), pltpu.VMEM((nh, 1, tq), F32), pltpu.VMEM((nh, 1, tq), F32),
                        pltpu.VMEM((nh, 1, tq), F32), pltpu.VMEM((nh, 1, tq), F32),
                        pltpu.VMEM((nh, MLA_V, tq), F32)],
        compiler_params=_params("parallel", "parallel"),
        name="mla_attention",
    )(qn, qr, kf, vt)


def _hgrn_proj_kernel(h_ref, g_ref, win_ref, lb_ref, q_ref, f_ref, v_ref, gate_ref):
    hk = HGRN_HEADS * HGRN_DK
    tm = h_ref.shape[0]
    lb = lb_ref[...]
    groups = [slice(k * tm // ROW_GROUPS, (k + 1) * tm // ROW_GROUPS) for k in range(ROW_GROUPS)]
    a = [_rms(h_ref[r, :], g_ref[...]).astype(BF16) for r in groups]
    for r, ak in zip(groups, a):
        qx = _dot(ak, win_ref[:, :hk])
        q_ref[r, :] = qx * jax.nn.sigmoid(qx)
    for r, ak in zip(groups, a):
        fx = _dot(ak, win_ref[:, hk:2 * hk])
        f_ref[r, :] = lb + (1.0 - lb) * jax.nn.sigmoid(fx)
    for r, ak in zip(groups, a):
        v_ref[r, :] = _dot(ak, win_ref[:, 2 * hk:3 * hk]).astype(BF16)
    for r, ak in zip(groups, a):
        gx = _dot(ak, win_ref[:, 3 * hk:])
        gate_ref[r, :] = (gx * jax.nn.sigmoid(gx)).astype(BF16)


def _hgrn_proj(h, gain, w_in, slot, lb):
    t = h.shape[0]
    tm = ROW_TILE
    hk = HGRN_HEADS * HGRN_DK
    row = pl.BlockSpec((tm, hk), lambda i: (i, 0))
    return pl.pallas_call(
        _hgrn_proj_kernel,
        grid=(t // tm,),
        in_specs=[row, _const_spec(gain.shape), _layer_spec(w_in, slot), _const_spec(lb.shape)],
        out_specs=[row, row, row, row],
        out_shape=[jax.ShapeDtypeStruct((t, hk), F32), jax.ShapeDtypeStruct((t, hk), F32),
                   jax.ShapeDtypeStruct((t, hk), BF16), jax.ShapeDtypeStruct((t, hk), BF16)],
        compiler_params=_params("parallel"),
        name="hgrn_proj",
    )(h, gain, w_in, lb)


def _chunk_cumsum(x, pos):
    shift = 1
    while shift < HGRN_CHUNK:
        x = x + jnp.where(pos >= shift, pltpu.roll(x, shift, 0), 0.0)
        shift *= 2
    return x


def _hgrn_rec_kernel(q_ref, f_ref, v_ref, gate_ref, on_ref, y_ref, state_ref):
    c = HGRN_CHUNK
    tl = HGRN_TILE
    nsub = tl // c

    @pl.when(pl.program_id(2) == 0)
    def _():
        state_ref[...] = jnp.zeros_like(state_ref)

    heads = range(HGRN_STEP_HEADS)
    chunks = [slice(n * c, (n + 1) * c) for n in range(nsub)]
    pos = lax.broadcasted_iota(jnp.int32, (tl, HGRN_DK), 0) % c
    row = lax.broadcasted_iota(jnp.int32, (tl, tl), 0)
    col = lax.broadcasted_iota(jnp.int32, (tl, tl), 1)
    keep = (row // c == col // c) & (row >= col)

    vs, q_decs, decays, scores, kvts = [], [], [], [], []
    for hd in heads:
        cols = slice(hd * HGRN_DK, (hd + 1) * HGRN_DK)
        q = q_ref[0, :, cols]
        f = f_ref[0, :, cols]
        v = v_ref[0, :, cols]
        k = 1.0 - f
        b = _chunk_cumsum(jnp.log2(f), pos)

        def per_chunk(fn, b=b):
            return jnp.concatenate([jnp.broadcast_to(fn(b[rows, :]), (c, HGRN_DK)) for rows in chunks], axis=0)

        b_ref = per_chunk(lambda bc: bc[c // 2:c // 2 + 1])
        q_rel = q * jnp.exp2(b - b_ref)
        k_rel = k * jnp.exp2(b_ref - b)
        q_dec = q_rel * per_chunk(lambda bc: jnp.exp2(bc[c // 2:c // 2 + 1]))
        k_dec = (k_rel * per_chunk(lambda bc: jnp.exp2(bc[c - 1:c] - bc[c // 2:c // 2 + 1]))).astype(BF16)
        scores.append(_dot(q_rel.astype(BF16), k_rel.T.astype(BF16)))
        kvts.append([_dot_tn(v[rows], k_dec[rows]) for rows in chunks])
        q_decs.append(q_dec.astype(BF16))
        decays.append([jnp.exp2(b[rows.stop - 1:rows.stop, :]) for rows in chunks])
        vs.append(v)

    for hd in heads:
        st = state_ref[hd]
        states = []
        for n in range(nsub):
            states.append(st.T.astype(BF16))
            st = st * decays[hd][n] + kvts[hd][n]
        state_ref[hd] = st
        o = _dot(jnp.where(keep, scores[hd], 0.0).astype(BF16), vs[hd])
        o = o + jnp.concatenate([_dot(q_decs[hd][rows], states[n]) for n, rows in enumerate(chunks)], axis=0)
        cols = slice(hd * HGRN_DV, (hd + 1) * HGRN_DV)
        y = _rms(o, on_ref[...]) * gate_ref[0, :, cols].astype(F32)
        y_ref[0, :, cols] = y.astype(y_ref.dtype)


def _hgrn_rec(q, f, v, gate, o_gain):
    b, s, _ = q.shape
    nh = HGRN_STEP_HEADS
    blk = pl.BlockSpec((1, HGRN_TILE, nh * HGRN_DK), lambda bi, hg, i: (bi, i, hg))
    return pl.pallas_call(
        _hgrn_rec_kernel,
        grid=(b, HGRN_HEADS // nh, s // HGRN_TILE),
        in_specs=[blk, blk, blk, blk, _const_spec(o_gain.shape)],
        out_specs=blk,
        out_shape=jax.ShapeDtypeStruct((b, s, HGRN_HEADS * HGRN_DV), BF16),
        scratch_shapes=[pltpu.VMEM((nh, HGRN_DV, HGRN_DK), F32)],
        compiler_params=_params("parallel", "parallel", "arbitrary"),
        name="hgrn_recurrence",
    )(q, f, v, gate, o_gain)


def _post_mlp_kernel(h_ref, y_ref, wo_ref, g_ref, w1_ref, w2_ref, out_ref, u_ref):
    g = g_ref[...]
    tm = h_ref.shape[0]
    groups = [slice(k * tm // ROW_GROUPS, (k + 1) * tm // ROW_GROUPS) for k in range(ROW_GROUPS)]
    mixed = [_dot(y_ref[r, :], wo_ref[...]) for r in groups]
    h1 = [h_ref[r, :] + _rms(m, g[1:2]) for r, m in zip(groups, mixed)]
    a = [_rms(x, g[2:3]).astype(BF16) for x in h1]
    for c in range(D_FF // FF_CHUNK):
        cols = slice(c * FF_CHUNK, (c + 1) * FF_CHUNK)
        for r, ak in zip(groups, a):
            u = jnp.maximum(_dot(ak, w1_ref[:, cols]), 0.0)
            u_ref[r, cols] = (u * u).astype(BF16)
    down = [_dot(u_ref[r, :], w2_ref[...]) for r in groups]
    for r, x, d in zip(groups, h1, down):
        out_ref[r, :] = x + _rms(d, g[3:4])


def _post_mlp(h, y, w_o, slot, gains, w1, w2, layer):
    t = h.shape[0]
    tm = MLP_ROW_TILE
    row = pl.BlockSpec((tm, D_MODEL), lambda i: (i, 0))
    return pl.pallas_call(
        _post_mlp_kernel,
        grid=(t // tm,),
        in_specs=[row, row, _layer_spec(w_o, slot), _layer_spec(gains, layer), _layer_spec(w1, layer),
                  _layer_spec(w2, layer)],
        out_specs=row,
        out_shape=jax.ShapeDtypeStruct((t, D_MODEL), F32),
        scratch_shapes=[pltpu.VMEM((tm, D_FF), BF16)],
        compiler_params=_params("parallel"),
        name="post_mlp",
    )(h, y, w_o, gains, w1, w2)


def _mla_weights(w_in, w_uq, w_ukv):
    n, nq, nkv = w_in.shape[0], MLA_Q_LORA, MLA_KV_LORA
    kr = w_in[:, :, nq + nkv:]
    w_in_x = jnp.concatenate([w_in[:, :, :nq + nkv], kr, kr], axis=2).astype(BF16)
    uq = w_uq.reshape(n, nq, MLA_HEADS, MLA_NOPE + MLA_ROPE)
    w_uq_x = jnp.concatenate([uq[..., :MLA_NOPE].reshape(n, nq, -1), uq[..., MLA_NOPE:].reshape(n, nq, -1)],
                             axis=2).astype(BF16)
    ukv = w_ukv.reshape(n, nkv, MLA_HEADS, MLA_NOPE + MLA_V)
    w_ukv_x = jnp.concatenate([ukv[..., :MLA_NOPE].reshape(n, nkv, -1), ukv[..., MLA_NOPE:].reshape(n, nkv, -1)],
                              axis=2).astype(BF16)
    return w_in_x, w_uq_x, w_ukv_x


def kernel(x, positions, norm_gains, mla_w_in, mla_q_norm, mla_kv_norm, mla_w_uq, mla_w_ukv, mla_w_o, hgrn_w_in,
           hgrn_lb_logits, hgrn_o_norm, hgrn_w_o, mlp_w1, mlp_w2):
    bsz, seq, d = x.shape
    t = bsz * seq
    assert d == D_MODEL and seq % max(ROW_TILE, MLP_ROW_TILE, ATTN_TQ, HGRN_TILE) == 0
    assert ROW_TILE % ATTN_TK == 0 and HGRN_TILE % HGRN_CHUNK == 0 and HGRN_HEADS % HGRN_STEP_HEADS == 0

    inv_freq = jnp.power(ROPE_BASE, -jnp.arange(0, MLA_ROPE, 2, dtype=F32) / MLA_ROPE)
    ang = positions.astype(F32).reshape(t, 1) * inv_freq
    ang = ang.reshape(-1, LANES)
    cos, sin = lax.optimization_barrier((jnp.cos(ang), jnp.sin(ang)))
    cos, sin = cos.reshape(t, -1), sin.reshape(t, -1)
    cos_t, sin_t = cos, sin
    p = jax.nn.softmax(hgrn_lb_logits.astype(F32), axis=0)
    lower_bounds = jnp.cumsum(p, axis=0) - p[0]
    q_scale = (MLA_NOPE + MLA_ROPE) ** -0.5 * math.log2(math.e)

    mla_w_o, hgrn_w_in, hgrn_w_o, mlp_w1, mlp_w2 = (
        w.astype(BF16) for w in (mla_w_o, hgrn_w_in, hgrn_w_o, mlp_w1, mlp_w2))
    mla_w_in, mla_w_uq, mla_w_ukv = _mla_weights(mla_w_in, mla_w_uq, mla_w_ukv)

    h = x.reshape(t, d)
    for layer in range(DEPTH):
        slot = layer // N_MIXERS
        if layer % N_MIXERS == 0:
            qn, qr, kf, vt = _mla_proj(h, norm_gains[layer, 0:1], mla_w_in, mla_q_norm[slot][None],
                                       mla_kv_norm[slot][None], mla_w_uq, mla_w_ukv, slot, cos_t, sin_t,
                                       q_scale, bsz)
            y = _attention(qn.reshape(bsz, seq, -1), qr.reshape(bsz, seq, -1), kf,
                           vt.reshape(bsz, seq // ATTN_TK, -1, ATTN_TK))
            w_o = mla_w_o
        else:
            q, f, v, gate = _hgrn_proj(h, norm_gains[layer, 0:1], hgrn_w_in, slot, lower_bounds[layer][None])
            shp = (bsz, seq, -1)
            y = _hgrn_rec(q.reshape(shp), f.reshape(shp), v.reshape(shp), gate.reshape(shp),
                          hgrn_o_norm[slot][None])
            w_o = hgrn_w_o
        h = _post_mlp(h, y.reshape(t, -1), w_o, slot, norm_gains, mlp_w1, mlp_w2, layer)
    return h.reshape(bsz, seq, d)
```
